```python
import jax, jax.numpy as jnp
from jax import lax
import numpy as np

D_MODEL = 1024
BATCH = 16
SEQ = 2048
DEPTH = 1

CHUNK = 64
ROPE_THETA = 10000.0
NORM_EPS = 1e-6

MLA_HEADS = 8
MLA_Q_LORA = 384
MLA_KV_LORA = 256
MLA_NOPE = 128
MLA_ROPE = 64
MLA_V = 128
MLA_QK = MLA_NOPE + MLA_ROPE
MLA_Q_BLOCK = 128

DSA_HEADS = 8
DSA_HEAD_DIM = 128
IDX_HEADS = 8
IDX_HEAD_DIM = 64
IDX_ROPE = 32
DSA_TOPK_MAX = 256
DSA_Q_BLOCK = 16

N_BRANCHES = 2

IN_SPLIT_SIZES = (
    MLA_Q_LORA,
    MLA_KV_LORA,
    MLA_ROPE,
    3 * DSA_HEADS * DSA_HEAD_DIM,
    IDX_HEADS * IDX_HEAD_DIM,
    IDX_HEAD_DIM,
    IDX_HEADS,
    N_BRANCHES * D_MODEL,
)
D_IN = (MLA_Q_LORA + MLA_KV_LORA + MLA_ROPE + 3 * DSA_HEADS * DSA_HEAD_DIM
        + IDX_HEADS * IDX_HEAD_DIM + IDX_HEAD_DIM + IDX_HEADS + N_BRANCHES * D_MODEL)

N_EXPERTS = 32
MOE_TOP_K = 4
D_EXPERT = 1024
SWIGLU_LIMIT = 7.0
SWIGLU_ALPHA = 1.702
MOE_BLOCK = 128

kernel_name = "hybrid_mla_dsa_gated_moe_block"


def rms_norm(x, g):
    xf = x.astype(jnp.float32)
    y = xf * lax.rsqrt(jnp.mean(xf * xf, axis=-1, keepdims=True) + NORM_EPS)
    return (y * g.astype(jnp.float32)).astype(x.dtype)


def rope(x, pos, rot_dim):
    half = rot_dim // 2
    inv_freq = ROPE_THETA ** (-jnp.arange(half, dtype=jnp.float32) / half)
    ang = pos.astype(jnp.float32)[:, None] * inv_freq[None, :]
    cos = jnp.cos(ang)[:, None, :].astype(x.dtype)
    sin = jnp.sin(ang)[:, None, :].astype(x.dtype)
    x1, x2, xp = x[..., :half], x[..., half:rot_dim], x[..., rot_dim:]
    return jnp.concatenate([x1 * cos - x2 * sin, x2 * cos + x1 * sin, xp], axis=-1)


def split_cols(proj):
    points, acc = [], 0
    for s in IN_SPLIT_SIZES[:-1]:
        acc += s
        points.append(acc)
    return jnp.split(proj, points, axis=-1)


def mla_attention(c_q, c_kv, k_r, q_lora_g, kv_lora_g, w_uq, w_uk, w_uv, pos):
    B, S, _ = c_q.shape
    cq = rms_norm(c_q, q_lora_g)
    ckv = rms_norm(c_kv, kv_lora_g)
    q = (cq @ w_uq).reshape(B, S, MLA_HEADS, MLA_QK)
    q = jnp.concatenate([q[..., :MLA_NOPE], rope(q[..., MLA_NOPE:], pos, MLA_ROPE)], axis=-1)
    q = q * (MLA_QK ** -0.5)
    k_nope = (ckv @ w_uk).reshape(B, S, MLA_HEADS, MLA_NOPE)
    k_rope = rope(k_r[:, :, None, :], pos, MLA_ROPE)
    k = jnp.concatenate([k_nope, jnp.broadcast_to(k_rope, (B, S, MLA_HEADS, MLA_ROPE))], axis=-1)
    v = (ckv @ w_uv).reshape(B, S, MLA_HEADS, MLA_V)
    key_chunk = pos // CHUNK
    n_blocks = S // MLA_Q_BLOCK

    def block(i):
        q_blk = lax.dynamic_slice_in_dim(q, i * MLA_Q_BLOCK, MLA_Q_BLOCK, axis=1)
        q_chunk = lax.dynamic_slice_in_dim(key_chunk, i * MLA_Q_BLOCK, MLA_Q_BLOCK)
        s = jnp.einsum('bqhd,bkhd->bhqk', q_blk, k).astype(jnp.float32)
        mask = key_chunk[None, :] <= q_chunk[:, None]
        s = jnp.where(mask[None, None], s, -jnp.inf)
        p = jax.nn.softmax(s, axis=-1).astype(v.dtype)
        return jnp.einsum('bhqk,bkhd->bqhd', p, v)

    o = lax.map(block, jnp.arange(n_blocks))
    return o.transpose(1, 0, 2, 3, 4).reshape(B, S, MLA_HEADS * MLA_V)


def dsa_attention(qkv, q_idx, k_idx, w_idx, pos, top_k):
    B, S, _ = qkv.shape
    q, k, v = jnp.split(qkv.reshape(B, S, 3, DSA_HEADS, DSA_HEAD_DIM), 3, axis=2)
    q, k, v = q[:, :, 0], k[:, :, 0], v[:, :, 0]
    q = rope(q, pos, DSA_HEAD_DIM) * (DSA_HEAD_DIM ** -0.5)
    k = rope(k, pos, DSA_HEAD_DIM)
    qi = rope(q_idx.reshape(B, S, IDX_HEADS, IDX_HEAD_DIM), pos, IDX_ROPE) * (IDX_HEAD_DIM ** -0.5)
    ki = rope(k_idx[:, :, None, :], pos, IDX_ROPE)[:, :, 0, :]
    wi = w_idx * (IDX_HEADS ** -0.5)
    key_chunk = pos // CHUNK
    n_blocks = S // DSA_Q_BLOCK
    gather_rows = jax.vmap(lambda kb, ib: kb[ib])

    def block(i):
        start = i * DSA_Q_BLOCK
        qi_b = lax.dynamic_slice_in_dim(qi, start, DSA_Q_BLOCK, axis=1)
        wi_b = lax.dynamic_slice_in_dim(wi, start, DSA_Q_BLOCK, axis=1)
        q_b = lax.dynamic_slice_in_dim(q, start, DSA_Q_BLOCK, axis=1)
        q_chunk = lax.dynamic_slice_in_dim(key_chunk, start, DSA_Q_BLOCK)
        raw = jnp.einsum('bqhd,bsd->bqhs', qi_b, ki)
        score = jnp.einsum('bqh,bqhs->bqs', wi_b, jax.nn.relu(raw)).astype(jnp.float32)
        admissible = key_chunk[None, :] <= q_chunk[:, None]
        score = jnp.where(admissible[None], score, -jnp.inf)
        _, sel = lax.top_k(score, top_k)
        valid = (sel // CHUNK) <= q_chunk[None, :, None]
        k_sel = gather_rows(k, sel)
        v_sel = gather_rows(v, sel)
        s = jnp.einsum('bqhd,bqkhd->bqhk', q_b, k_sel).astype(jnp.float32)
        s = jnp.where(valid[:, :, None, :], s, -jnp.inf)
        p = jax.nn.softmax(s, axis=-1).astype(v.dtype)
        return jnp.einsum('bqhk,bqkhd->bqhd', p, v_sel)

    o = lax.map(block, jnp.arange(n_blocks))
    return o.transpose(1, 0, 2, 3, 4).reshape(B, S, DSA_HEADS * DSA_HEAD_DIM)


def moe_ffn(xn, w_router, b_router, w_gate_up, b_gate_up, w_down, b_down):
    B, S, D = xn.shape
    T = B * S
    xt = xn.reshape(T, D)
    logits = (xt @ w_router + b_router).astype(jnp.float32)
    top_vals, top_idx = lax.top_k(logits, MOE_TOP_K)
    gates = jax.nn.softmax(top_vals, axis=-1)
    A = T * MOE_TOP_K
    e_flat = top_idx.reshape(A)
    tok_flat = jnp.arange(A, dtype=jnp.int32) // MOE_TOP_K
    g_flat = gates.reshape(A)
    order = jnp.argsort(e_flat)
    e_s, tok_s, g_s = e_flat[order], tok_flat[order], g_flat[order]
    counts = jax.ops.segment_sum(jnp.ones((A,), jnp.int32), e_flat, num_segments=N_EXPERTS)
    offsets = jnp.cumsum(counts) - counts
    padded = ((counts + MOE_BLOCK - 1) // MOE_BLOCK) * MOE_BLOCK
    pad_end = jnp.cumsum(padded)
    pad_off = pad_end - padded
    dest = pad_off[e_s] + jnp.arange(A, dtype=jnp.int32) - offsets[e_s]
    P = A + N_EXPERTS * MOE_BLOCK
    n_blocks = P // MOE_BLOCK
    tok_buf = jnp.full((P,), T, jnp.int32).at[dest].set(tok_s)
    g_buf = jnp.zeros((P,), jnp.float32).at[dest].set(g_s)
    blk_start = jnp.arange(n_blocks, dtype=jnp.int32) * MOE_BLOCK
    blk_expert = jnp.minimum(jnp.searchsorted(pad_end, blk_start, side='right'), N_EXPERTS - 1)
    x_pad = jnp.concatenate([xt, jnp.zeros((1, D), xt.dtype)], axis=0)

    def expert_block(args):
        toks, e = args
        xb = x_pad[toks]
        h = xb @ w_gate_up[e] + b_gate_up[e]
        gate = jnp.minimum(h[:, :D_EXPERT], SWIGLU_LIMIT)
        lin = jnp.clip(h[:, D_EXPERT:], -SWIGLU_LIMIT, SWIGLU_LIMIT)
        glu = gate * jax.nn.sigmoid(SWIGLU_ALPHA * gate)
        return ((lin + 1.0) * glu) @ w_down[e] + b_down[e]

    y = lax.map(expert_block, (tok_buf.reshape(n_blocks, MOE_BLOCK), blk_expert))
    y = y.reshape(P, D) * g_buf[:, None].astype(y.dtype)
    out = jnp.zeros((T + 1, D), y.dtype).at[tok_buf].add(y)[:T]
    return out.reshape(B, S, D)


def setup_inputs(seed: int = 0) -> dict:
    key = jax.random.key(seed)
    ks = jax.random.split(key, 20)

    def w(k, shape, fan_in):
        return jax.random.normal(k, shape, jnp.float32) * (fan_in ** -0.5)

    def gain(k, shape):
        return 1.0 + 0.05 * jax.random.normal(k, shape, jnp.float32)

    def bias(k, shape, scale=0.01):
        return scale * jax.random.normal(k, shape, jnp.float32)

    L = DEPTH
    return {
        "x": jax.random.normal(ks[0], (BATCH, SEQ, D_MODEL), jnp.float32),
        "norm_attn_g": gain(ks[1], (L, D_MODEL)),
        "w_in": w(ks[2], (L, D_MODEL, D_IN), D_MODEL),
        "q_lora_g": gain(ks[3], (L, MLA_Q_LORA)),
        "kv_lora_g": gain(ks[4], (L, MLA_KV_LORA)),
        "w_uq": w(ks[5], (L, MLA_Q_LORA, MLA_HEADS * MLA_QK), MLA_Q_LORA),
        "w_uk": w(ks[6], (L, MLA_KV_LORA, MLA_HEADS * MLA_NOPE), MLA_KV_LORA),
        "w_uv": w(ks[7], (L, MLA_KV_LORA, MLA_HEADS * MLA_V), MLA_KV_LORA),
        "w_o_mla": w(ks[8], (L, MLA_HEADS * MLA_V, D_MODEL), MLA_HEADS * MLA_V),
        "w_o_dsa": w(ks[9], (L, DSA_HEADS * DSA_HEAD_DIM, D_MODEL), DSA_HEADS * DSA_HEAD_DIM),
        "b_gate": bias(ks[10], (L, N_BRANCHES * D_MODEL), 0.1),
        "w_out": w(ks[11], (L, D_MODEL, D_MODEL), D_MODEL),
        "norm_ffn_g": gain(ks[12], (L, D_MODEL)),
        "w_router": w(ks[13], (L, D_MODEL, N_EXPERTS), D_MODEL),
        "b_router": bias(ks[14], (L, N_EXPERTS)),
        "w_gate_up": w(ks[15], (L, N_EXPERTS, D_MODEL, 2 * D_EXPERT), D_MODEL),
        "b_gate_up": bias(ks[16], (L, N_EXPERTS, 2 * D_EXPERT)),
        "w_down": w(ks[17], (L, N_EXPERTS, D_EXPERT, D_MODEL), D_EXPERT),
        "b_down": bias(ks[18], (L, N_EXPERTS, D_MODEL)),
        "norm_final_g": gain(ks[19], (D_MODEL,)),
    }


def reference(x, norm_attn_g, w_in, q_lora_g, kv_lora_g, w_uq, w_uk, w_uv, w_o_mla, w_o_dsa,
              b_gate, w_out, norm_ffn_g, w_router, b_router, w_gate_up, b_gate_up, w_down,
              b_down, norm_final_g):
    B, S, D = x.shape
    pos = jnp.arange(S, dtype=jnp.int32)
    top_k = min(DSA_TOPK_MAX, S // 4)
    for l in range(DEPTH):
        h = rms_norm(x, norm_attn_g[l])
        proj = h @ w_in[l]
        c_q, c_kv, k_r, qkv, q_idx, k_idx, w_idx, gate_cols = split_cols(proj)
        y_a = mla_attention(c_q, c_kv, k_r, q_lora_g[l], kv_lora_g[l],
                            w_uq[l], w_uk[l], w_uv[l], pos) @ w_o_mla[l]
        y_b = dsa_attention(qkv, q_idx, k_idx, w_idx, pos, top_k) @ w_o_dsa[l]
        g = jax.nn.sigmoid(gate_cols + b_gate[l]).reshape(B, S, N_BRANCHES, D)
        merged = g[:, :, 0, :] * y_a + g[:, :, 1, :] * y_b
        x = x + merged @ w_out[l]
        hn = rms_norm(x, norm_ffn_g[l])
        x = x + moe_ffn(hn, w_router[l], b_router[l], w_gate_up[l], b_gate_up[l],
                        w_down[l], b_down[l])
    return rms_norm(x, norm_final_g)
```

```python
import functools

import jax
import jax.numpy as jnp
from jax import lax
from jax.experimental import pallas as pl
from jax.experimental.pallas import tpu as pltpu

D_MODEL = 1024
CHUNK = 64
ROPE_THETA = 10000.0
NORM_EPS = 1e-6

MLA_HEADS = 8
MLA_Q_LORA = 384
MLA_KV_LORA = 256
MLA_NOPE = 128
MLA_ROPE = 64
MLA_V = 128
MLA_QK = MLA_NOPE + MLA_ROPE
MLA_QK_PAD = 256

DSA_HEADS = 8
DSA_HEAD_DIM = 128
IDX_HEADS = 8
IDX_HEAD_DIM = 64
IDX_ROPE = 32
DSA_TOPK_MAX = 256

N_EXPERTS = 32
MOE_TOP_K = 4
D_EXPERT = 1024
SWIGLU_LIMIT = 7.0
SWIGLU_ALPHA = 1.702

LANES = 128
MOE_BLOCK = 256
ROW_TILE = (D_MODEL // LANES, LANES)
VMEM_LIMIT = 56 * 1024 * 1024

OFF_GATES = 0
OFF_Q = 2048
OFF_K = 3072
OFF_V = 4096
OFF_QIDX = 5120
OFF_KIDX = 5632
OFF_CQ = 5760
OFF_CKV = 6144
OFF_KR = 6400
D_IN_PACKED = 6528

NEG = -1e30
INT_MIN = -(2 ** 31)

_NT = (((1,), (1,)), ((), ()))


def _cparams(sem):
    return pltpu.CompilerParams(dimension_semantics=sem, vmem_limit_bytes=VMEM_LIMIT)


def _rope3(x, c, sa, sb, shift):
    n = x.shape[-1]
    return x * c + pltpu.roll(x, shift, 1) * sa + pltpu.roll(x, n - shift, 1) * sb


def _inproj_body(x_ref, g_ref, w_ref, o_ref, h_scr):
    @pl.when(pl.program_id(1) == 0)
    def _():
        x = x_ref[...]
        ms = jnp.mean(x * x, axis=-1, keepdims=True)
        h_scr[...] = (x * lax.rsqrt(ms + NORM_EPS) * g_ref[...]).astype(jnp.bfloat16)

    o_ref[...] = jnp.dot(h_scr[...], w_ref[...], preferred_element_type=jnp.float32).astype(o_ref.dtype)


def _in_proj(x2, g, w_packed, tm, tn):
    T = x2.shape[0]
    return pl.pallas_call(
        _inproj_body,
        grid=(T // tm, D_IN_PACKED // tn),
        in_specs=[
            pl.BlockSpec((tm, D_MODEL), lambda i, j: (i, 0)),
            pl.BlockSpec((1, D_MODEL), lambda i, j: (0, 0)),
            pl.BlockSpec((D_MODEL, tn), lambda i, j: (0, j)),
        ],
        out_specs=pl.BlockSpec((tm, tn), lambda i, j: (i, j)),
        out_shape=jax.ShapeDtypeStruct((T, D_IN_PACKED), jnp.bfloat16),
        scratch_shapes=[pltpu.VMEM((tm, D_MODEL), jnp.bfloat16)],
        compiler_params=_cparams(("arbitrary", "arbitrary")),
        name="in_proj",
    )(x2, g, w_packed)


def _mla_prep_body(cq_ref, ckv_ref, kr_ref, gq_ref, gkv_ref, wq_ref, wk_ref, wv_ref,
                   c_ref, sa_ref, sb_ref, q_ref, k_ref, v_ref):
    def norm(ref, g_ref):
        x = ref[...].astype(jnp.float32)
        ms = jnp.mean(x * x, axis=-1, keepdims=True)
        return (x * lax.rsqrt(ms + NORM_EPS) * g_ref[...]).astype(jnp.bfloat16)

    cq = norm(cq_ref, gq_ref)
    ckv = norm(ckv_ref, gkv_ref)
    c, sa, sb = c_ref[...], sa_ref[...], sb_ref[...]
    scale = MLA_QK ** -0.5
    q_all = jnp.dot(cq, wq_ref[...], preferred_element_type=jnp.float32)
    kn = jnp.dot(ckv, wk_ref[...], preferred_element_type=jnp.float32)
    v_ref[...] = jnp.dot(ckv, wv_ref[...], preferred_element_type=jnp.float32).astype(v_ref.dtype)
    kr = _rope3(kr_ref[...].astype(jnp.float32), c, sa, sb, MLA_ROPE // 2).astype(k_ref.dtype)
    for h in range(MLA_HEADS):
        o = h * MLA_QK_PAD
        q_ref[:, o:o + LANES] = (q_all[:, o:o + LANES] * scale).astype(q_ref.dtype)
        qr = _rope3(q_all[:, o + LANES:o + 2 * LANES], c, sa, sb, MLA_ROPE // 2)
        q_ref[:, o + LANES:o + 2 * LANES] = (qr * scale).astype(q_ref.dtype)
        k_ref[:, o:o + LANES] = kn[:, h * LANES:(h + 1) * LANES].astype(k_ref.dtype)
        k_ref[:, o + LANES:o + 2 * LANES] = kr


def _mla_prep(proj, gq, gkv, wq, wk, wv, tabs, S, tm):
    T = proj.shape[0]
    ns = S // tm
    full = lambda shape: pl.BlockSpec(shape, lambda i: (0, 0))
    tab = pl.BlockSpec((tm, LANES), lambda i: (i % ns, 0))
    hq = MLA_HEADS * MLA_QK_PAD
    return pl.pallas_call(
        _mla_prep_body,
        grid=(T // tm,),
        in_specs=[
            pl.BlockSpec((tm, MLA_Q_LORA), lambda i: (i, OFF_CQ // MLA_Q_LORA)),
            pl.BlockSpec((tm, MLA_KV_LORA), lambda i: (i, OFF_CKV // MLA_KV_LORA)),
            pl.BlockSpec((tm, LANES), lambda i: (i, OFF_KR // LANES)),
            full((1, MLA_Q_LORA)), full((1, MLA_KV_LORA)),
            full((MLA_Q_LORA, hq)), full((MLA_KV_LORA, MLA_HEADS * MLA_NOPE)),
            full((MLA_KV_LORA, MLA_HEADS * MLA_V)),
            tab, tab, tab,
        ],
        out_specs=[
            pl.BlockSpec((tm, hq), lambda i: (i, 0)),
            pl.BlockSpec((tm, hq), lambda i: (i, 0)),
            pl.BlockSpec((tm, MLA_HEADS * MLA_V), lambda i: (i, 0)),
        ],
        out_shape=[
            jax.ShapeDtypeStruct((T, hq), jnp.bfloat16),
            jax.ShapeDtypeStruct((T, hq), jnp.bfloat16),
            jax.ShapeDtypeStruct((T, MLA_HEADS * MLA_V), jnp.bfloat16),
        ],
        compiler_params=_cparams(("arbitrary",)),
        name="mla_prep",
    )(proj, proj, proj, gq, gkv, wq, wk, wv, *tabs)


def _mla_attn_body(q_ref, k_ref, v_ref, o_ref, *, tq):
    i = pl.program_id(2)
    q = q_ref[...]

    def step(j, carry, masked):
        m, l, acc = carry
        off = pl.multiple_of(j * tq, tq)
        kb = k_ref[pl.ds(off, tq), :]
        vb = v_ref[pl.ds(off, tq), :]
        s = lax.dot_general(q, kb, _NT, preferred_element_type=jnp.float32)
        if masked:
            qc = lax.broadcasted_iota(jnp.int32, (tq, tq), 0) // CHUNK
            kc = lax.broadcasted_iota(jnp.int32, (tq, tq), 1) // CHUNK
            s = jnp.where(kc <= qc, s, NEG)
        m_new = jnp.maximum(m, jnp.max(s, axis=-1, keepdims=True))
        alpha = jnp.exp(m - m_new)
        p = jnp.exp(s - m_new)
        l = alpha * l + jnp.sum(p, axis=-1, keepdims=True)
        acc = alpha * acc + jnp.dot(p.astype(vb.dtype), vb, preferred_element_type=jnp.float32)
        return m_new, l, acc

    init = (jnp.full((tq, 1), NEG, jnp.float32), jnp.zeros((tq, 1), jnp.float32),
            jnp.zeros((tq, MLA_V), jnp.float32))
    carry = lax.fori_loop(0, i, lambda j, c: step(j, c, False), init)
    _, l, acc = step(i, carry, True)
    o_ref[...] = (acc / l).astype(o_ref.dtype)


def _mla_attn(q, k, v, B, S, tq):
    T = B * S
    nq = S // tq
    return pl.pallas_call(
        functools.partial(_mla_attn_body, tq=tq),
        grid=(B, MLA_HEADS, nq),
        in_specs=[
            pl.BlockSpec((tq, MLA_QK_PAD), lambda b, h, i: (b * nq + i, h)),
            pl.BlockSpec((S, MLA_QK_PAD), lambda b, h, i: (b, h)),
            pl.BlockSpec((S, MLA_V), lambda b, h, i: (b, h)),
        ],
        out_specs=pl.BlockSpec((tq, MLA_V), lambda b, h, i: (b * nq + i, h)),
        out_shape=jax.ShapeDtypeStruct((T, MLA_HEADS * MLA_V), jnp.bfloat16),
        compiler_params=_cparams(("arbitrary", "arbitrary", "arbitrary")),
        name="mla_attn",
    )(q, k, v)


def _dsa_prep_body(q_ref, k_ref, qi_ref, kx_ref, cd_ref, sd_ref, ci_ref, sai_ref, sbi_ref,
                   ck_ref, sak_ref, sbk_ref, qo_ref, ko_ref, qio_ref, kia_ref, kib_ref, wi_ref):
    cd, sd = cd_ref[...], sd_ref[...]
    for h in range(DSA_HEADS):
        sl = slice(h * LANES, (h + 1) * LANES)
        xq = q_ref[:, sl].astype(jnp.float32)
        xk = k_ref[:, sl].astype(jnp.float32)
        rq = xq * cd + pltpu.roll(xq, DSA_HEAD_DIM // 2, 1) * sd
        rk = xk * cd + pltpu.roll(xk, DSA_HEAD_DIM // 2, 1) * sd
        qo_ref[:, sl] = (rq * (DSA_HEAD_DIM ** -0.5)).astype(qo_ref.dtype)
        ko_ref[:, sl] = rk.astype(ko_ref.dtype)
    ci, sai, sbi = ci_ref[...], sai_ref[...], sbi_ref[...]
    for p in range(IDX_HEADS * IDX_HEAD_DIM // LANES):
        sl = slice(p * LANES, (p + 1) * LANES)
        r = _rope3(qi_ref[:, sl].astype(jnp.float32), ci, sai, sbi, IDX_ROPE // 2)
        qio_ref[:, sl] = (r * (IDX_HEAD_DIM ** -0.5)).astype(qio_ref.dtype)
    kx = kx_ref[...].astype(jnp.float32)
    kr = _rope3(kx, ck_ref[...], sak_ref[...], sbk_ref[...], IDX_ROPE // 2)
    lane = lax.broadcasted_iota(jnp.int32, kr.shape, 1)
    ka = jnp.where(lane < IDX_HEAD_DIM, kr, 0.0)
    kia_ref[...] = ka.astype(kia_ref.dtype)
    kib_ref[...] = pltpu.roll(ka, IDX_HEAD_DIM, 1).astype(kib_ref.dtype)
    wi_ref[...] = kx * (IDX_HEADS ** -0.5)


def _dsa_prep(proj, tabs, S, tm):
    T = proj.shape[0]
    ns = S // tm
    tab = pl.BlockSpec((tm, LANES), lambda i: (i % ns, 0))
    hd = DSA_HEADS * DSA_HEAD_DIM
    hi = IDX_HEADS * IDX_HEAD_DIM
    row = lambda w, off: pl.BlockSpec((tm, w), lambda i: (i, off // w))
    out = lambda w: pl.BlockSpec((tm, w), lambda i: (i, 0))
    return pl.pallas_call(
        _dsa_prep_body,
        grid=(T // tm,),
        in_specs=[row(hd, OFF_Q), row(hd, OFF_K), row(hi, OFF_QIDX), row(LANES, OFF_KIDX)] + [tab] * 8,
        out_specs=[out(hd), out(hd), out(hi), out(LANES), out(LANES), out(LANES)],
        out_shape=[
            jax.ShapeDtypeStruct((T, hd), jnp.bfloat16),
            jax.ShapeDtypeStruct((T, hd), jnp.bfloat16),
            jax.ShapeDtypeStruct((T, hi), jnp.bfloat16),
            jax.ShapeDtypeStruct((T, LANES), jnp.bfloat16),
            jax.ShapeDtypeStruct((T, LANES), jnp.bfloat16),
            jax.ShapeDtypeStruct((T, LANES), jnp.float32),
        ],
        compiler_params=_cparams(("arbitrary",)),
        name="dsa_prep",
    )(proj, proj, proj, proj, *tabs)


def _dsa_attn_body(qd_ref, qi_ref, wi_ref, kd_ref, v_ref, kia_ref, kib_ref, o_ref, key_scr, bias_scr,
                   *, tq, tk, top_k, idx_bits):
    i = pl.program_id(1)
    q0 = i * tq
    nkv = (q0 + tq + tk - 1) // tk
    qchunk = (q0 + lax.broadcasted_iota(jnp.int32, (tq, 1), 0)) // CHUNK
    lane_k = lax.broadcasted_iota(jnp.int32, (tq, tk), 1)
    wi = wi_ref[...]

    def idx_block(j, _):
        off = pl.multiple_of(j * tk, tk)
        ka = kia_ref[pl.ds(off, tk), :]
        kb = kib_ref[pl.ds(off, tk), :]
        score = jnp.zeros((tq, tk), jnp.float32)
        for p in range(IDX_HEADS // 2):
            qp = qi_ref[:, p * LANES:(p + 1) * LANES]
            ra = lax.dot_general(qp, ka, _NT, preferred_element_type=jnp.float32)
            rb = lax.dot_general(qp, kb, _NT, preferred_element_type=jnp.float32)
            c0 = IDX_HEAD_DIM + 2 * p
            score = score + wi[:, c0:c0 + 1] * jnp.maximum(ra, 0.0)
            score = score + wi[:, c0 + 1:c0 + 2] * jnp.maximum(rb, 0.0)
        adm = ((off + lane_k) // CHUNK) <= qchunk
        score = jnp.where(adm, score + 0.0, -jnp.inf)
        b = pltpu.bitcast(score, jnp.int32)
        key_scr[j] = b ^ ((b >> 31) & 0x7FFFFFFF)
        return 0

    lax.fori_loop(0, nkv, idx_block, 0)

    def count(pred):
        def blk(j, part):
            c = pred(key_scr[j], j * tk + lane_k).astype(jnp.int32)
            for l in range(tk // LANES):
                part = part + c[:, l * LANES:(l + 1) * LANES]
            return part
        part = lax.fori_loop(0, nkv, blk, jnp.zeros((tq, LANES), jnp.int32))
        return jnp.sum(part, axis=1, keepdims=True)

    tau = jnp.where(count(lambda k, _: k >= 0) >= top_k, 0, INT_MIN).astype(jnp.int32)

    def tau_bit(it, tau):
        cand = tau | (jnp.int32(1) << (30 - it))
        return jnp.where(count(lambda k, _: k >= cand) >= top_k, cand, tau)

    tau = lax.fori_loop(0, 31, tau_bit, tau)
    need = top_k - count(lambda k, _: k > tau)

    def tie_bit(it, mi):
        step = jnp.int32(1) << (idx_bits - 1 - it)
        c = count(lambda k, idx: (k == tau) & (idx <= mi + step - 1))
        return jnp.where(c < need, mi + step, mi)

    mi = lax.fori_loop(0, idx_bits, tie_bit, jnp.zeros((tq, 1), jnp.int32))

    def bias_block(j, _):
        k = key_scr[j]
        idx = j * tk + lane_k
        sel = (k > tau) | ((k == tau) & (idx <= mi))
        adm = (idx // CHUNK) <= qchunk
        bias_scr[j] = jnp.where(sel & adm, 0.0, NEG)
        return 0

    lax.fori_loop(0, nkv, bias_block, 0)

    for h in range(DSA_HEADS):
        sl = slice(h * DSA_HEAD_DIM, (h + 1) * DSA_HEAD_DIM)
        qh = qd_ref[:, sl]

        def att(j, carry, sl=sl, qh=qh):
            m, l, acc = carry
            off = pl.multiple_of(j * tk, tk)
            kb = kd_ref[pl.ds(off, tk), sl]
            vb = v_ref[pl.ds(off, tk), sl]
            s = lax.dot_general(qh, kb, _NT, preferred_element_type=jnp.float32) + bias_scr[j]
            m_new = jnp.maximum(m, jnp.max(s, axis=-1, keepdims=True))
            alpha = jnp.exp(m - m_new)
            p = jnp.exp(s - m_new)
            l = alpha * l + jnp.sum(p, axis=-1, keepdims=True)
            acc = alpha * acc + jnp.dot(p.astype(vb.dtype), vb, preferred_element_type=jnp.float32)
            return m_new, l, acc

        init = (jnp.full((tq, 1), NEG, jnp.float32), jnp.zeros((tq, 1), jnp.float32),
                jnp.zeros((tq, DSA_HEAD_DIM), jnp.float32))
        _, l, acc = lax.fori_loop(0, nkv, att, init)
        o_ref[:, sl] = (acc / l).astype(o_ref.dtype)


def _dsa_attn(qd, qi, wi, kd, proj, kia, kib, B, S, tq, tk, top_k):
    T = B * S
    nq = S // tq
    hd = DSA_HEADS * DSA_HEAD_DIM
    hi = IDX_HEADS * IDX_HEAD_DIM
    idx_bits = max(1, (S - 1).bit_length())
    qrow = lambda w: pl.BlockSpec((tq, w), lambda b, i: (b * nq + i, 0))
    seq = lambda w, col: pl.BlockSpec((S, w), lambda b, i: (b, col))
    return pl.pallas_call(
        functools.partial(_dsa_attn_body, tq=tq, tk=tk, top_k=top_k, idx_bits=idx_bits),
        grid=(B, nq),
        in_specs=[qrow(hd), qrow(hi), qrow(LANES), seq(hd, 0), seq(hd, OFF_V // hd), seq(LANES, 0), seq(LANES, 0)],
        out_specs=qrow(hd),
        out_shape=jax.ShapeDtypeStruct((T, hd), jnp.bfloat16),
        scratch_shapes=[pltpu.VMEM((S // tk, tq, tk), jnp.int32), pltpu.VMEM((S // tk, tq, tk), jnp.float32)],
        compiler_params=_cparams(("arbitrary", "arbitrary")),
        name="dsa_attn",
    )(qd, qi, wi, kd, proj, kia, kib)


def _merge_body(oa_ref, ob_ref, gc_ref, x_ref, woa_ref, wob_ref, wout_ref, bg_ref, gf_ref, wr_ref, br_ref,
                x1_ref, hn_ref, lg_ref):
    ya = jnp.dot(oa_ref[...], woa_ref[...], preferred_element_type=jnp.float32)
    yb = jnp.dot(ob_ref[...], wob_ref[...], preferred_element_type=jnp.float32)
    g = jax.nn.sigmoid(gc_ref[...].astype(jnp.float32) + bg_ref[...])
    merged = g[:, :D_MODEL] * ya + g[:, D_MODEL:] * yb
    x1 = x_ref[...] + jnp.dot(merged.astype(jnp.bfloat16), wout_ref[...], preferred_element_type=jnp.float32)
    x1_ref[...] = x1
    ms = jnp.mean(x1 * x1, axis=-1, keepdims=True)
    hn = x1 * lax.rsqrt(ms + NORM_EPS) * gf_ref[...]
    hn_ref[...] = hn.reshape(hn_ref.shape)
    lg_ref[...] = lax.dot_general(wr_ref[...], hn.astype(jnp.bfloat16), _NT,
                                  preferred_element_type=jnp.float32) + br_ref[...]


def _merge(oa, ob, proj, x2, woa, wob, wout, bg, gf, wr_t, br, tm):
    T = x2.shape[0]
    row = lambda w: pl.BlockSpec((tm, w), lambda i: (i, 0))
    full = lambda shape: pl.BlockSpec(shape, lambda i: (0, 0))
    return pl.pallas_call(
        _merge_body,
        grid=(T // tm,),
        in_specs=[row(D_MODEL), row(D_MODEL), row(2 * D_MODEL), row(D_MODEL),
                  full((D_MODEL, D_MODEL)), full((D_MODEL, D_MODEL)), full((D_MODEL, D_MODEL)),
                  full((1, 2 * D_MODEL)), full((1, D_MODEL)), full((N_EXPERTS, D_MODEL)), full((N_EXPERTS, 1))],
        out_specs=[row(D_MODEL), pl.BlockSpec((tm,) + ROW_TILE, lambda i: (i, 0, 0)),
                   pl.BlockSpec((N_EXPERTS, tm), lambda i: (0, i))],
        out_shape=[
            jax.ShapeDtypeStruct((T, D_MODEL), jnp.float32),
            jax.ShapeDtypeStruct((T,) + ROW_TILE, jnp.float32),
            jax.ShapeDtypeStruct((N_EXPERTS, T), jnp.float32),
        ],
        compiler_params=_cparams(("arbitrary",)),
        name="merge",
    )(oa, ob, proj, x2, woa, wob, wout, bg, gf, wr_t, br)


def _route_body(lg_ref, eidx_ref, gate_ref, rank_ref, cnt_ref, carry_scr, *, tr):
    @pl.when(pl.program_id(0) == 0)
    def _():
        carry_scr[...] = jnp.zeros_like(carry_scr)

    vals = lg_ref[...]
    rows = lax.broadcasted_iota(jnp.int32, vals.shape, 0)
    tops, hots = [], []
    for k in range(MOE_TOP_K):
        m = jnp.max(vals, axis=0, keepdims=True)
        idx = jnp.min(jnp.where(vals == m, rows, N_EXPERTS), axis=0, keepdims=True)
        hot = rows == idx
        vals = jnp.where(hot, -jnp.inf, vals)
        tops.append(m)
        hots.append(hot)
        eidx_ref[k:k + 1, :] = idx
    es = [jnp.exp(m - tops[0]) for m in tops]
    den = es[0] + es[1] + es[2] + es[3]
    for k in range(MOE_TOP_K):
        gate_ref[k:k + 1, :] = es[k] / den
    sel = jnp.zeros(vals.shape, jnp.float32)
    for hot in hots:
        sel = sel + hot.astype(jnp.float32)
    before = (lax.broadcasted_iota(jnp.int32, (tr, tr), 0) < lax.broadcasted_iota(jnp.int32, (tr, tr), 1))
    prefix = jnp.dot(sel.astype(jnp.bfloat16), before.astype(jnp.bfloat16), preferred_element_type=jnp.float32)
    prefix = prefix + carry_scr[:, 0:1]
    for k in range(MOE_TOP_K):
        rank_ref[k:k + 1, :] = jnp.sum(jnp.where(hots[k], prefix, 0.0), axis=0, keepdims=True).astype(jnp.int32)
    carry_scr[...] = carry_scr[...] + jnp.sum(sel, axis=1, keepdims=True)
    cnt_ref[...] = carry_scr[...]


def _route(logits_t, tr):
    T = logits_t.shape[1]
    tok = pl.BlockSpec((MOE_TOP_K, tr), lambda i: (0, i))
    return pl.pallas_call(
        functools.partial(_route_body, tr=tr),
        grid=(T // tr,),
        in_specs=[pl.BlockSpec((N_EXPERTS, tr), lambda i: (0, i))],
        out_specs=[tok, tok, tok, pl.BlockSpec((N_EXPERTS, LANES), lambda i: (0, 0))],
        out_shape=[
            jax.ShapeDtypeStruct((MOE_TOP_K, T), jnp.int32),
            jax.ShapeDtypeStruct((MOE_TOP_K, T), jnp.float32),
            jax.ShapeDtypeStruct((MOE_TOP_K, T), jnp.int32),
            jax.ShapeDtypeStruct((N_EXPERTS, LANES), jnp.float32),
        ],
        scratch_shapes=[pltpu.VMEM((N_EXPERTS, LANES), jnp.float32)],
        compiler_params=_cparams(("arbitrary",)),
        name="route",
    )(logits_t)


def _dest_body(poff_ref, eidx_ref, rank_ref, dest_ref):
    e = eidx_ref[...]
    base = jnp.zeros(e.shape, jnp.int32)
    for x in range(N_EXPERTS):
        base = jnp.where(e == x, poff_ref[x], base)
    dest_ref[...] = base + rank_ref[...]


def _dest(pad_off, eidx, rank, tr):
    T = eidx.shape[1]
    tok = pl.BlockSpec((MOE_TOP_K, tr), lambda i, p: (0, i))
    return pl.pallas_call(
        _dest_body,
        grid_spec=pltpu.PrefetchScalarGridSpec(
            num_scalar_prefetch=1, grid=(T // tr,), in_specs=[tok, tok], out_specs=tok),
        out_shape=jax.ShapeDtypeStruct((MOE_TOP_K, T), jnp.int32),
        compiler_params=_cparams(("arbitrary",)),
        name="dest",
    )(pad_off, eidx, rank)


def _dispatch_body(zstart_ref, zvalid_ref, dest_ref, hn_ref, xs_ref, z_scr, zsem, sem, *, tm):
    @pl.when(pl.program_id(0) == 0)
    def _():
        z_scr[...] = jnp.zeros_like(z_scr)

        def zero_copy(n):
            return pltpu.make_async_copy(z_scr, xs_ref.at[pl.ds(zstart_ref[n], MOE_BLOCK)], zsem)

        def zissue(n, _):
            @pl.when(zvalid_ref[n] > 0)
            def _():
                zero_copy(n).start()
            return 0

        def zdrain(n, _):
            @pl.when(zvalid_ref[n] > 0)
            def _():
                zero_copy(n).wait()
            return 0

        lax.fori_loop(0, 2 * N_EXPERTS, zissue, 0)
        lax.fori_loop(0, 2 * N_EXPERTS, zdrain, 0)

    t0 = pl.program_id(0) * tm

    def row_copy(t, d):
        return pltpu.make_async_copy(hn_ref.at[t0 + t], xs_ref.at[d], sem)

    def issue(t, _):
        for k in range(MOE_TOP_K):
            row_copy(t, dest_ref[k, t]).start()
        return 0

    lax.fori_loop(0, tm, issue, 0)

    def drain(t, _):
        for k in range(MOE_TOP_K):
            row_copy(t, dest_ref[k, t]).wait()
        return 0

    lax.fori_loop(0, tm, drain, 0)


def _dispatch(zero_start, zero_valid, dest, hn, P, tm):
    T = hn.shape[0]
    return pl.pallas_call(
        functools.partial(_dispatch_body, tm=tm),
        grid_spec=pltpu.PrefetchScalarGridSpec(
            num_scalar_prefetch=2, grid=(T // tm,),
            in_specs=[
                pl.BlockSpec((MOE_TOP_K, tm), lambda i, zs, zv: (0, i), memory_space=pltpu.SMEM),
                pl.BlockSpec(memory_space=pl.ANY),
            ],
            out_specs=pl.BlockSpec(memory_space=pl.ANY),
            scratch_shapes=[pltpu.VMEM((MOE_BLOCK,) + ROW_TILE, hn.dtype), pltpu.SemaphoreType.DMA(()),
                            pltpu.SemaphoreType.DMA(())]),
        out_shape=jax.ShapeDtypeStruct((P,) + ROW_TILE, hn.dtype),
        compiler_params=_cparams(("arbitrary",)),
        name="dispatch",
    )(zero_start, zero_valid, dest, hn)


def _experts_body(bexp_ref, first_ref, nused_ref, xs_ref, wgu_ref, bgu_ref, wd_ref, bd_ref, y_ref, wgu_s, wd_s):
    j = pl.program_id(0)

    @pl.when(first_ref[j] > 0)
    def _():
        wgu_s[...] = wgu_ref[0].astype(jnp.bfloat16)
        wd_s[...] = wd_ref[0].astype(jnp.bfloat16)

    @pl.when(j < nused_ref[0])
    def _():
        xb = xs_ref[...].reshape(MOE_BLOCK, D_MODEL).astype(jnp.bfloat16)
        h = jnp.dot(xb, wgu_s[...], preferred_element_type=jnp.float32) + bgu_ref[0]
        gate = jnp.minimum(h[:, :D_EXPERT], SWIGLU_LIMIT)
        lin = jnp.clip(h[:, D_EXPERT:], -SWIGLU_LIMIT, SWIGLU_LIMIT)
        glu = gate * jax.nn.sigmoid(SWIGLU_ALPHA * gate)
        act = ((lin + 1.0) * glu).astype(jnp.bfloat16)
        y = jnp.dot(act, wd_s[...], preferred_element_type=jnp.float32) + bd_ref[0]
        y_ref[...] = y.reshape(y_ref.shape)

    @pl.when(j >= nused_ref[0])
    def _():
        y_ref[...] = jnp.zeros_like(y_ref)


def _experts(blk_expert, blk_first, n_used, xs, wgu, bgu, wd, bd):
    P = xs.shape[0]
    nb = P // MOE_BLOCK
    rows = pl.BlockSpec((MOE_BLOCK,) + ROW_TILE, lambda j, be, bf, nu: (j, 0, 0))
    return pl.pallas_call(
        _experts_body,
        grid_spec=pltpu.PrefetchScalarGridSpec(
            num_scalar_prefetch=3, grid=(nb,),
            in_specs=[
                rows,
                pl.BlockSpec((1, D_MODEL, 2 * D_EXPERT), lambda j, be, bf, nu: (be[j], 0, 0)),
                pl.BlockSpec((1, 1, 2 * D_EXPERT), lambda j, be, bf, nu: (be[j], 0, 0)),
                pl.BlockSpec((1, D_EXPERT, D_MODEL), lambda j, be, bf, nu: (be[j], 0, 0)),
                pl.BlockSpec((1, 1, D_MODEL), lambda j, be, bf, nu: (be[j], 0, 0)),
            ],
            out_specs=rows,
            scratch_shapes=[pltpu.VMEM((D_MODEL, 2 * D_EXPERT), jnp.bfloat16),
                            pltpu.VMEM((D_EXPERT, D_MODEL), jnp.bfloat16)]),
        out_shape=jax.ShapeDtypeStruct((P,) + ROW_TILE, jnp.float32),
        compiler_params=_cparams(("arbitrary",)),
        name="experts",
    )(blk_expert, blk_first, n_used, xs, wgu, bgu, wd, bd)


def _combine_body(dest_ref, y_ref, x1_ref, gate_ref, gn_ref, o_ref, ybuf, sem, *, tm):
    def row_copy(k, t, d):
        return pltpu.make_async_copy(y_ref.at[d], ybuf.at[k, t], sem)

    def issue(t, _):
        for k in range(MOE_TOP_K):
            row_copy(k, t, dest_ref[k, t]).start()
        return 0

    lax.fori_loop(0, tm, issue, 0)

    def drain(t, _):
        for k in range(MOE_TOP_K):
            row_copy(k, t, dest_ref[k, t]).wait()
        return 0

    lax.fori_loop(0, tm, drain, 0)

    g = gate_ref[...]
    acc = x1_ref[...]
    for k in range(MOE_TOP_K):
        acc = acc + g[:, k:k + 1] * ybuf[k].reshape(tm, D_MODEL)
    ms = jnp.mean(acc * acc, axis=-1, keepdims=True)
    o_ref[...] = acc * lax.rsqrt(ms + NORM_EPS) * gn_ref[...]


def _combine(dest, y, x1, gate_t, gn, tm):
    T = x1.shape[0]
    return pl.pallas_call(
        functools.partial(_combine_body, tm=tm),
        grid=(T // tm,),
        in_specs=[
            pl.BlockSpec((MOE_TOP_K, tm), lambda i: (0, i), memory_space=pltpu.SMEM),
            pl.BlockSpec(memory_space=pl.ANY),
            pl.BlockSpec((tm, D_MODEL), lambda i: (i, 0)),
            pl.BlockSpec((tm, MOE_TOP_K), lambda i: (i, 0)),
            pl.BlockSpec((1, D_MODEL), lambda i: (0, 0)),
        ],
        out_specs=pl.BlockSpec((tm, D_MODEL), lambda i: (i, 0)),
        out_shape=jax.ShapeDtypeStruct((T, D_MODEL), jnp.float32),
        scratch_shapes=[pltpu.VMEM((MOE_TOP_K, tm) + ROW_TILE, jnp.float32), pltpu.SemaphoreType.DMA(())],
        compiler_params=_cparams(("arbitrary",)),
        name="combine",
    )(dest, y, x1, gate_t, gn)


def _rope_tables(S):
    pos = jnp.arange(S, dtype=jnp.float32)[:, None]

    def cs(half):
        inv = ROPE_THETA ** (-jnp.arange(half, dtype=jnp.float32) / half)
        ang = pos * inv[None, :]
        return jnp.cos(ang), jnp.sin(ang)

    z = lambda n: jnp.zeros((S, n), jnp.float32)
    o = lambda n: jnp.ones((S, n), jnp.float32)
    cat = lambda *a: jnp.concatenate(a, axis=1)
    c, s = cs(MLA_ROPE // 2)
    mla = (cat(c, c, z(64)), cat(z(32), s, z(64)), cat(-s, z(96)))
    c, s = cs(DSA_HEAD_DIM // 2)
    dsa = (cat(c, c), cat(-s, s))
    c, s = cs(IDX_ROPE // 2)
    c1, sa1, sb1 = cat(c, c, o(32)), cat(z(16), s, z(32)), cat(-s, z(48))
    idx_q = (cat(c1, c1), cat(sa1, sa1), cat(sb1, sb1))
    idx_k = (cat(c1, o(64)), cat(sa1, z(64)), cat(sb1, z(64)))
    return mla, dsa + idx_q + idx_k


def _pack_w_in(w):
    sizes = (MLA_Q_LORA, MLA_KV_LORA, MLA_ROPE, 3 * DSA_HEADS * DSA_HEAD_DIM, IDX_HEADS * IDX_HEAD_DIM,
             IDX_HEAD_DIM, IDX_HEADS, 2 * D_MODEL)
    pts, acc = [], 0
    for s in sizes[:-1]:
        acc += s
        pts.append(acc)
    c_q, c_kv, k_r, qkv, q_idx, k_idx, w_idx, gates = jnp.split(w, pts, axis=1)
    z = lambda n: jnp.zeros((w.shape[0], n), w.dtype)
    packed = jnp.concatenate([gates, qkv, q_idx, k_idx, w_idx, z(LANES - IDX_HEAD_DIM - IDX_HEADS),
                              c_q, c_kv, k_r, z(LANES - MLA_ROPE)], axis=1)
    assert packed.shape[1] == D_IN_PACKED
    return packed.astype(jnp.bfloat16)


def _pack_w_uq(w):
    w = w.reshape(MLA_Q_LORA, MLA_HEADS, MLA_QK)
    w = jnp.pad(w, ((0, 0), (0, 0), (0, MLA_QK_PAD - MLA_QK)))
    return w.reshape(MLA_Q_LORA, MLA_HEADS * MLA_QK_PAD).astype(jnp.bfloat16)


def _tile(n, pref):
    t = min(n, pref)
    assert n % t == 0
    return t


def kernel(x, norm_attn_g, w_in, q_lora_g, kv_lora_g, w_uq, w_uk, w_uv, w_o_mla, w_o_dsa, b_gate, w_out,
           norm_ffn_g, w_router, b_router, w_gate_up, b_gate_up, w_down, b_down, norm_final_g):
    B, S, D = x.shape
    assert D == D_MODEL and S % CHUNK == 0 and norm_attn_g.shape[0] == 1
    T = B * S
    top_k = min(DSA_TOPK_MAX, S // 4)
    bf = jnp.bfloat16
    x2 = x.reshape(T, D)
    mla_tabs, dsa_tabs = _rope_tables(S)

    proj = _in_proj(x2, norm_attn_g, _pack_w_in(w_in[0]), _tile(T, 1024), D_IN_PACKED // 3)
    tm = _tile(S, 512)
    q, k, v = _mla_prep(proj, q_lora_g, kv_lora_g, _pack_w_uq(w_uq[0]), w_uk[0].astype(bf), w_uv[0].astype(bf),
                        mla_tabs, S, tm)
    o_mla = _mla_attn(q, k, v, B, S, _tile(S, 256))
    qd, kd, qi, kia, kib, wi = _dsa_prep(proj, dsa_tabs, S, tm)
    o_dsa = _dsa_attn(qd, qi, wi, kd, proj, kia, kib, B, S, _tile(S, 128), _tile(S, 512), top_k)
    x1, hn, logits_t = _merge(o_mla, o_dsa, proj, x2, w_o_mla[0].astype(bf), w_o_dsa[0].astype(bf),
                              w_out[0].astype(bf), b_gate, norm_ffn_g, w_router[0].T.astype(bf),
                              b_router[0][:, None], _tile(T, 512))

    tr = _tile(T, 512)
    eidx, gate, rank, cnt = _route(logits_t, tr)
    counts = cnt[:, 0].astype(jnp.int32)
    padded = ((counts + MOE_BLOCK - 1) // MOE_BLOCK) * MOE_BLOCK
    pad_end = jnp.cumsum(padded)
    pad_off = pad_end - padded
    P = T * MOE_TOP_K + N_EXPERTS * MOE_BLOCK
    nb = P // MOE_BLOCK
    blk_start = jnp.arange(nb, dtype=jnp.int32) * MOE_BLOCK
    blk_expert = jnp.minimum(jnp.searchsorted(pad_end, blk_start, side="right"), N_EXPERTS - 1).astype(jnp.int32)
    blk_first = jnp.concatenate([jnp.ones((1,), jnp.int32), (blk_expert[1:] != blk_expert[:-1]).astype(jnp.int32)])
    n_used = (pad_end[-1:] // MOE_BLOCK).astype(jnp.int32)

    dest = _dest(pad_off.astype(jnp.int32), eidx, rank, tr)
    trailing = n_used[0] + jnp.arange(N_EXPERTS, dtype=jnp.int32)
    zero_start = jnp.concatenate([pad_end - MOE_BLOCK, jnp.minimum(trailing, nb - 1) * MOE_BLOCK]).astype(jnp.int32)
    zero_valid = jnp.concatenate([padded > 0, trailing < nb]).astype(jnp.int32)
    xs = _dispatch(zero_start, zero_valid, dest, hn, P, _tile(T, 256))
    y = _experts(blk_expert, blk_first, n_used, xs, w_gate_up[0], b_gate_up[0][:, None, :],
                 w_down[0], b_down[0][:, None, :])
    out = _combine(dest, y, x1, gate.T, norm_final_g[None, :], _tile(T, 128))
    return out.reshape(B, S, D)
```

```python
import functools

import jax
import jax.numpy as jnp
from jax import lax
from jax.experimental import pallas as pl
from jax.experimental.pallas import tpu as pltpu

D_MODEL = 1024
CHUNK = 64
ROPE_THETA = 10000.0
NORM_EPS = 1e-6

MLA_HEADS = 8
MLA_Q_LORA = 384
MLA_KV_LORA = 256
MLA_NOPE = 128
MLA_ROPE = 64
MLA_V = 128
MLA_QK = MLA_NOPE + MLA_ROPE
MLA_QK_PAD = 256

DSA_HEADS = 8
DSA_HEAD_DIM = 128
IDX_HEADS = 8
IDX_HEAD_DIM = 64
IDX_ROPE = 32
DSA_TOPK_MAX = 256

N_EXPERTS = 32
MOE_TOP_K = 4
D_EXPERT = 1024
SWIGLU_LIMIT = 7.0
SWIGLU_ALPHA = 1.702

LANES = 128
MOE_BLOCK = 256
ROW_TILE = (D_MODEL // LANES, LANES)
VMEM_LIMIT = 56 * 1024 * 1024

OFF_GATES = 0
OFF_Q = 2048
OFF_K = 3072
OFF_V = 4096
OFF_QIDX = 5120
OFF_KIDX = 5632
OFF_CQ = 5760
OFF_CKV = 6144
OFF_KR = 6400
D_IN_PACKED = 6528

NEG = -1e30
INT_MIN = -(2 ** 31)

_NT = (((1,), (1,)), ((), ()))


def _cparams(sem):
    return pltpu.CompilerParams(dimension_semantics=sem, vmem_limit_bytes=VMEM_LIMIT)


def _rope3(x, c, sa, sb, shift):
    n = x.shape[-1]
    return x * c + pltpu.roll(x, shift, 1) * sa + pltpu.roll(x, n - shift, 1) * sb


def _inproj_body(x_ref, g_ref, w_ref, o_ref, h_scr):
    @pl.when(pl.program_id(1) == 0)
    def _():
        x = x_ref[...]
        ms = jnp.mean(x * x, axis=-1, keepdims=True)
        h_scr[...] = (x * lax.rsqrt(ms + NORM_EPS) * g_ref[...]).astype(jnp.bfloat16)

    o_ref[...] = jnp.dot(h_scr[...], w_ref[...], preferred_element_type=jnp.float32).astype(o_ref.dtype)


def _in_proj(x2, g, w_packed, tm, tn):
    T = x2.shape[0]
    return pl.pallas_call(
        _inproj_body,
        grid=(T // tm, D_IN_PACKED // tn),
        in_specs=[
            pl.BlockSpec((tm, D_MODEL), lambda i, j: (i, 0)),
            pl.BlockSpec((1, D_MODEL), lambda i, j: (0, 0)),
            pl.BlockSpec((D_MODEL, tn), lambda i, j: (0, j)),
        ],
        out_specs=pl.BlockSpec((tm, tn), lambda i, j: (i, j)),
        out_shape=jax.ShapeDtypeStruct((T, D_IN_PACKED), jnp.bfloat16),
        scratch_shapes=[pltpu.VMEM((tm, D_MODEL), jnp.bfloat16)],
        compiler_params=_cparams(("arbitrary", "arbitrary")),
        name="in_proj",
    )(x2, g, w_packed)


def _mla_prep_body(cq_ref, ckv_ref, kr_ref, gq_ref, gkv_ref, wq_ref, wk_ref, wv_ref,
                   c_ref, sa_ref, sb_ref, q_ref, k_ref, v_ref):
    def norm(ref, g_ref):
        x = ref[...].astype(jnp.float32)
        ms = jnp.mean(x * x, axis=-1, keepdims=True)
        return (x * lax.rsqrt(ms + NORM_EPS) * g_ref[...]).astype(jnp.bfloat16)

    cq = norm(cq_ref, gq_ref)
    ckv = norm(ckv_ref, gkv_ref)
    c, sa, sb = c_ref[...], sa_ref[...], sb_ref[...]
    scale = MLA_QK ** -0.5
    q_all = jnp.dot(cq, wq_ref[...], preferred_element_type=jnp.float32)
    kn = jnp.dot(ckv, wk_ref[...], preferred_element_type=jnp.float32)
    v_ref[...] = jnp.dot(ckv, wv_ref[...], preferred_element_type=jnp.float32).astype(v_ref.dtype)
    kr = _rope3(kr_ref[...].astype(jnp.float32), c, sa, sb, MLA_ROPE // 2).astype(k_ref.dtype)
    for h in range(MLA_HEADS):
        o = h * MLA_QK_PAD
        q_ref[:, o:o + LANES] = (q_all[:, o:o + LANES] * scale).astype(q_ref.dtype)
        qr = _rope3(q_all[:, o + LANES:o + 2 * LANES], c, sa, sb, MLA_ROPE // 2)
        q_ref[:, o + LANES:o + 2 * LANES] = (qr * scale).astype(q_ref.dtype)
        k_ref[:, o:o + LANES] = kn[:, h * LANES:(h + 1) * LANES].astype(k_ref.dtype)
        k_ref[:, o + LANES:o + 2 * LANES] = kr


def _mla_prep(proj, gq, gkv, wq, wk, wv, tabs, S, tm):
    T = proj.shape[0]
    ns = S // tm
    full = lambda shape: pl.BlockSpec(shape, lambda i: (0, 0))
    tab = pl.BlockSpec((tm, LANES), lambda i: (i % ns, 0))
    hq = MLA_HEADS * MLA_QK_PAD
    return pl.pallas_call(
        _mla_prep_body,
        grid=(T // tm,),
        in_specs=[
            pl.BlockSpec((tm, MLA_Q_LORA), lambda i: (i, OFF_CQ // MLA_Q_LORA)),
            pl.BlockSpec((tm, MLA_KV_LORA), lambda i: (i, OFF_CKV // MLA_KV_LORA)),
            pl.BlockSpec((tm, LANES), lambda i: (i, OFF_KR // LANES)),
            full((1, MLA_Q_LORA)), full((1, MLA_KV_LORA)),
            full((MLA_Q_LORA, hq)), full((MLA_KV_LORA, MLA_HEADS * MLA_NOPE)),
            full((MLA_KV_LORA, MLA_HEADS * MLA_V)),
            tab, tab, tab,
        ],
        out_specs=[
            pl.BlockSpec((tm, hq), lambda i: (i, 0)),
            pl.BlockSpec((tm, hq), lambda i: (i, 0)),
            pl.BlockSpec((tm, MLA_HEADS * MLA_V), lambda i: (i, 0)),
        ],
        out_shape=[
            jax.ShapeDtypeStruct((T, hq), jnp.bfloat16),
            jax.ShapeDtypeStruct((T, hq), jnp.bfloat16),
            jax.ShapeDtypeStruct((T, MLA_HEADS * MLA_V), jnp.bfloat16),
        ],
        compiler_params=_cparams(("arbitrary",)),
        name="mla_prep",
    )(proj, proj, proj, gq, gkv, wq, wk, wv, *tabs)


def _mla_attn_body(q_ref, k_ref, v_ref, o_ref, *, tq):
    i = pl.program_id(2)
    q = q_ref[...]

    def step(j, carry, masked):
        m, l, acc = carry
        off = pl.multiple_of(j * tq, tq)
        kb = k_ref[pl.ds(off, tq), :]
        vb = v_ref[pl.ds(off, tq), :]
        s = lax.dot_general(q, kb, _NT, preferred_element_type=jnp.float32)
        if masked:
            qc = lax.broadcasted_iota(jnp.int32, (tq, tq), 0) // CHUNK
            kc = lax.broadcasted_iota(jnp.int32, (tq, tq), 1) // CHUNK
            s = jnp.where(kc <= qc, s, NEG)
        m_new = jnp.maximum(m, jnp.max(s, axis=-1, keepdims=True))
        alpha = jnp.exp(m - m_new)
        p = jnp.exp(s - m_new)
        l = alpha * l + jnp.sum(p, axis=-1, keepdims=True)
        acc = alpha * acc + jnp.dot(p.astype(vb.dtype), vb, preferred_element_type=jnp.float32)
        return m_new, l, acc

    init = (jnp.full((tq, 1), NEG, jnp.float32), jnp.zeros((tq, 1), jnp.float32),
            jnp.zeros((tq, MLA_V), jnp.float32))
    carry = lax.fori_loop(0, i, lambda j, c: step(j, c, False), init)
    _, l, acc = step(i, carry, True)
    o_ref[...] = (acc / l).astype(o_ref.dtype)


def _mla_attn(q, k, v, B, S, tq):
    T = B * S
    nq = S // tq
    return pl.pallas_call(
        functools.partial(_mla_attn_body, tq=tq),
        grid=(B, MLA_HEADS, nq),
        in_specs=[
            pl.BlockSpec((tq, MLA_QK_PAD), lambda b, h, i: (b * nq + i, h)),
            pl.BlockSpec((S, MLA_QK_PAD), lambda b, h, i: (b, h)),
            pl.BlockSpec((S, MLA_V), lambda b, h, i: (b, h)),
        ],
        out_specs=pl.BlockSpec((tq, MLA_V), lambda b, h, i: (b * nq + i, h)),
        out_shape=jax.ShapeDtypeStruct((T, MLA_HEADS * MLA_V), jnp.bfloat16),
        compiler_params=_cparams(("arbitrary", "arbitrary", "arbitrary")),
        name="mla_attn",
    )(q, k, v)


def _dsa_prep_body(q_ref, k_ref, qi_ref, kx_ref, cd_ref, sd_ref, ci_ref, sai_ref, sbi_ref,
                   ck_ref, sak_ref, sbk_ref, qo_ref, ko_ref, qio_ref, kia_ref, kib_ref, wi_ref):
    cd, sd = cd_ref[...], sd_ref[...]
    for h in range(DSA_HEADS):
        sl = slice(h * LANES, (h + 1) * LANES)
        xq = q_ref[:, sl].astype(jnp.float32)
        xk = k_ref[:, sl].astype(jnp.float32)
        rq = xq * cd + pltpu.roll(xq, DSA_HEAD_DIM // 2, 1) * sd
        rk = xk * cd + pltpu.roll(xk, DSA_HEAD_DIM // 2, 1) * sd
        qo_ref[:, sl] = (rq * (DSA_HEAD_DIM ** -0.5)).astype(qo_ref.dtype)
        ko_ref[:, sl] = rk.astype(ko_ref.dtype)
    ci, sai, sbi = ci_ref[...], sai_ref[...], sbi_ref[...]
    for p in range(IDX_HEADS * IDX_HEAD_DIM // LANES):
        sl = slice(p * LANES, (p + 1) * LANES)
        r = _rope3(qi_ref[:, sl].astype(jnp.float32), ci, sai, sbi, IDX_ROPE // 2)
        qio_ref[:, sl] = (r * (IDX_HEAD_DIM ** -0.5)).astype(qio_ref.dtype)
    kx = kx_ref[...].astype(jnp.float32)
    kr = _rope3(kx, ck_ref[...], sak_ref[...], sbk_ref[...], IDX_ROPE // 2)
    lane = lax.broadcasted_iota(jnp.int32, kr.shape, 1)
    ka = jnp.where(lane < IDX_HEAD_DIM, kr, 0.0)
    kia_ref[...] = ka.astype(kia_ref.dtype)
    kib_ref[...] = pltpu.roll(ka, IDX_HEAD_DIM, 1).astype(kib_ref.dtype)
    wi_ref[...] = kx * (IDX_HEADS ** -0.5)


def _dsa_prep(proj, tabs, S, tm):
    T = proj.shape[0]
    ns = S // tm
    tab = pl.BlockSpec((tm, LANES), lambda i: (i % ns, 0))
    hd = DSA_HEADS * DSA_HEAD_DIM
    hi = IDX_HEADS * IDX_HEAD_DIM
    row = lambda w, off: pl.BlockSpec((tm, w), lambda i: (i, off // w))
    out = lambda w: pl.BlockSpec((tm, w), lambda i: (i, 0))
    return pl.pallas_call(
        _dsa_prep_body,
        grid=(T // tm,),
        in_specs=[row(hd, OFF_Q), row(hd, OFF_K), row(hi, OFF_QIDX), row(LANES, OFF_KIDX)] + [tab] * 8,
        out_specs=[out(hd), out(hd), out(hi), out(LANES), out(LANES), out(LANES)],
        out_shape=[
            jax.ShapeDtypeStruct((T, hd), jnp.bfloat16),
            jax.ShapeDtypeStruct((T, hd), jnp.bfloat16),
            jax.ShapeDtypeStruct((T, hi), jnp.bfloat16),
            jax.ShapeDtypeStruct((T, LANES), jnp.bfloat16),
            jax.ShapeDtypeStruct((T, LANES), jnp.bfloat16),
            jax.ShapeDtypeStruct((T, LANES), jnp.float32),
        ],
        compiler_params=_cparams(("arbitrary",)),
        name="dsa_prep",
    )(proj, proj, proj, proj, *tabs)


def _dsa_attn_body(qd_ref, qi_ref, wi_ref, kd_ref, v_ref, kia_ref, kib_ref, o_ref, key_scr, bias_scr,
                   *, tq, tk, top_k, idx_bits):
    i = pl.program_id(1)
    q0 = i * tq
    nkv = (q0 + tq + tk - 1) // tk
    qchunk = (q0 + lax.broadcasted_iota(jnp.int32, (tq, 1), 0)) // CHUNK
    lane_k = lax.broadcasted_iota(jnp.int32, (tq, tk), 1)
    wi = wi_ref[...]

    def idx_block(j, _):
        off = pl.multiple_of(j * tk, tk)
        ka = kia_ref[pl.ds(off, tk), :]
        kb = kib_ref[pl.ds(off, tk), :]
        score = jnp.zeros((tq, tk), jnp.float32)
        for p in range(IDX_HEADS // 2):
            qp = qi_ref[:, p * LANES:(p + 1) * LANES]
            ra = lax.dot_general(qp, ka, _NT, preferred_element_type=jnp.float32)
            rb = lax.dot_general(qp, kb, _NT, preferred_element_type=jnp.float32)
            c0 = IDX_HEAD_DIM + 2 * p
            score = score + wi[:, c0:c0 + 1] * jnp.maximum(ra, 0.0)
            score = score + wi[:, c0 + 1:c0 + 2] * jnp.maximum(rb, 0.0)
        adm = ((off + lane_k) // CHUNK) <= qchunk
        score = jnp.where(adm, score + 0.0, -jnp.inf)
        b = pltpu.bitcast(score, jnp.int32)
        key_scr[j] = b ^ ((b >> 31) & 0x7FFFFFFF)
        return 0

    lax.fori_loop(0, nkv, idx_block, 0)

    def count(pred):
        def blk(j, part):
            c = pred(key_scr[j], j * tk + lane_k).astype(jnp.int32)
            for l in range(tk // LANES):
                part = part + c[:, l * LANES:(l + 1) * LANES]
            return part
        part = lax.fori_loop(0, nkv, blk, jnp.zeros((tq, LANES), jnp.int32))
        return jnp.sum(part, axis=1, keepdims=True)

    tau = jnp.where(count(lambda k, _: k >= 0) >= top_k, 0, INT_MIN).astype(jnp.int32)

    def tau_bit(it, tau):
        cand = tau | (jnp.int32(1) << (30 - it))
        return jnp.where(count(lambda k, _: k >= cand) >= top_k, cand, tau)

    tau = lax.fori_loop(0, 31, tau_bit, tau)
    need = top_k - count(lambda k, _: k > tau)

    def tie_bit(it, mi):
        step = jnp.int32(1) << (idx_bits - 1 - it)
        c = count(lambda k, idx: (k == tau) & (idx <= mi + step - 1))
        return jnp.where(c < need, mi + step, mi)

    mi = lax.fori_loop(0, idx_bits, tie_bit, jnp.zeros((tq, 1), jnp.int32))

    def bias_block(j, _):
        k = key_scr[j]
        idx = j * tk + lane_k
        sel = (k > tau) | ((k == tau) & (idx <= mi))
        adm = (idx // CHUNK) <= qchunk
        bias_scr[j] = jnp.where(sel & adm, 0.0, NEG)
        return 0

    lax.fori_loop(0, nkv, bias_block, 0)

    for h in range(DSA_HEADS):
        sl = slice(h * DSA_HEAD_DIM, (h + 1) * DSA_HEAD_DIM)
        qh = qd_ref[:, sl]

        def att(j, carry, sl=sl, qh=qh):
            m, l, acc = carry
            off = pl.multiple_of(j * tk, tk)
            kb = kd_ref[pl.ds(off, tk), sl]
            vb = v_ref[pl.ds(off, tk), sl]
            s = lax.dot_general(qh, kb, _NT, preferred_element_type=jnp.float32) + bias_scr[j]
            m_new = jnp.maximum(m, jnp.max(s, axis=-1, keepdims=True))
            alpha = jnp.exp(m - m_new)
            p = jnp.exp(s - m_new)
            l = alpha * l + jnp.sum(p, axis=-1, keepdims=True)
            acc = alpha * acc + jnp.dot(p.astype(vb.dtype), vb, preferred_element_type=jnp.float32)
            return m_new, l, acc

        init = (jnp.full((tq, 1), NEG, jnp.float32), jnp.zeros((tq, 1), jnp.float32),
                jnp.zeros((tq, DSA_HEAD_DIM), jnp.float32))
        _, l, acc = lax.fori_loop(0, nkv, att, init)
        o_ref[:, sl] = (acc / l).astype(o_ref.dtype)


def _dsa_attn(qd, qi, wi, kd, proj, kia, kib, B, S, tq, tk, top_k):
    T = B * S
    nq = S // tq
    hd = DSA_HEADS * DSA_HEAD_DIM
    hi = IDX_HEADS * IDX_HEAD_DIM
    idx_bits = max(1, (S - 1).bit_length())
    qrow = lambda w: pl.BlockSpec((tq, w), lambda b, i: (b * nq + i, 0))
    seq = lambda w, col: pl.BlockSpec((S, w), lambda b, i: (b, col))
    return pl.pallas_call(
        functools.partial(_dsa_attn_body, tq=tq, tk=tk, top_k=top_k, idx_bits=idx_bits),
        grid=(B, nq),
        in_specs=[qrow(hd), qrow(hi), qrow(LANES), seq(hd, 0), seq(hd, OFF_V // hd), seq(LANES, 0), seq(LANES, 0)],
        out_specs=qrow(hd),
        out_shape=jax.ShapeDtypeStruct((T, hd), jnp.bfloat16),
        scratch_shapes=[pltpu.VMEM((S // tk, tq, tk), jnp.int32), pltpu.VMEM((S // tk, tq, tk), jnp.float32)],
        compiler_params=_cparams(("arbitrary", "arbitrary")),
        name="dsa_attn",
    )(qd, qi, wi, kd, proj, kia, kib)


def _merge_body(oa_ref, ob_ref, gc_ref, x_ref, woa_ref, wob_ref, wout_ref, bg_ref, gf_ref, wr_ref, br_ref,
                x1_ref, hn_ref, lg_ref):
    ya = jnp.dot(oa_ref[...], woa_ref[...], preferred_element_type=jnp.float32)
    yb = jnp.dot(ob_ref[...], wob_ref[...], preferred_element_type=jnp.float32)
    g = jax.nn.sigmoid(gc_ref[...].astype(jnp.float32) + bg_ref[...])
    merged = g[:, :D_MODEL] * ya + g[:, D_MODEL:] * yb
    x1 = x_ref[...] + jnp.dot(merged.astype(jnp.bfloat16), wout_ref[...], preferred_element_type=jnp.float32)
    x1_ref[...] = x1
    ms = jnp.mean(x1 * x1, axis=-1, keepdims=True)
    hn = x1 * lax.rsqrt(ms + NORM_EPS) * gf_ref[...]
    hn_ref[...] = hn.reshape(hn_ref.shape)
    lg_ref[...] = lax.dot_general(wr_ref[...], hn.astype(jnp.bfloat16), _NT,
                                  preferred_element_type=jnp.float32) + br_ref[...]


def _merge(oa, ob, proj, x2, woa, wob, wout, bg, gf, wr_t, br, tm):
    T = x2.shape[0]
    row = lambda w: pl.BlockSpec((tm, w), lambda i: (i, 0))
    full = lambda shape: pl.BlockSpec(shape, lambda i: (0, 0))
    return pl.pallas_call(
        _merge_body,
        grid=(T // tm,),
        in_specs=[row(D_MODEL), row(D_MODEL), row(2 * D_MODEL), row(D_MODEL),
                  full((D_MODEL, D_MODEL)), full((D_MODEL, D_MODEL)), full((D_MODEL, D_MODEL)),
                  full((1, 2 * D_MODEL)), full((1, D_MODEL)), full((N_EXPERTS, D_MODEL)), full((N_EXPERTS, 1))],
        out_specs=[row(D_MODEL), pl.BlockSpec((tm,) + ROW_TILE, lambda i: (i, 0, 0)),
                   pl.BlockSpec((N_EXPERTS, tm), lambda i: (0, i))],
        out_shape=[
            jax.ShapeDtypeStruct((T, D_MODEL), jnp.float32),
            jax.ShapeDtypeStruct((T,) + ROW_TILE, jnp.float32),
            jax.ShapeDtypeStruct((N_EXPERTS, T), jnp.float32),
        ],
        compiler_params=_cparams(("arbitrary",)),
        name="merge",
    )(oa, ob, proj, x2, woa, wob, wout, bg, gf, wr_t, br)


def _route_body(lg_ref, eidx_ref, gate_ref, rank_ref, cnt_ref, carry_scr, *, tr):
    @pl.when(pl.program_id(0) == 0)
    def _():
        carry_scr[...] = jnp.zeros_like(carry_scr)

    vals = lg_ref[...]
    rows = lax.broadcasted_iota(jnp.int32, vals.shape, 0)
    tops, hots = [], []
    for k in range(MOE_TOP_K):
        m = jnp.max(vals, axis=0, keepdims=True)
        idx = jnp.min(jnp.where(vals == m, rows, N_EXPERTS), axis=0, keepdims=True)
        hot = rows == idx
        vals = jnp.where(hot, -jnp.inf, vals)
        tops.append(m)
        hots.append(hot)
        eidx_ref[k:k + 1, :] = idx
    es = [jnp.exp(m - tops[0]) for m in tops]
    den = es[0] + es[1] + es[2] + es[3]
    for k in range(MOE_TOP_K):
        gate_ref[k:k + 1, :] = es[k] / den
    sel = jnp.zeros(vals.shape, jnp.float32)
    for hot in hots:
        sel = sel + hot.astype(jnp.float32)
    before = (lax.broadcasted_iota(jnp.int32, (tr, tr), 0) < lax.broadcasted_iota(jnp.int32, (tr, tr), 1))
    prefix = jnp.dot(sel.astype(jnp.bfloat16), before.astype(jnp.bfloat16), preferred_element_type=jnp.float32)
    prefix = prefix + carry_scr[:, 0:1]
    for k in range(MOE_TOP_K):
        rank_ref[k:k + 1, :] = jnp.sum(jnp.where(hots[k], prefix, 0.0), axis=0, keepdims=True).astype(jnp.int32)
    carry_scr[...] = carry_scr[...] + jnp.sum(sel, axis=1, keepdims=True)
    cnt_ref[...] = carry_scr[...]


def _route(logits_t, tr):
    T = logits_t.shape[1]
    tok = pl.BlockSpec((MOE_TOP_K, tr), lambda i: (0, i))
    return pl.pallas_call(
        functools.partial(_route_body, tr=tr),
        grid=(T // tr,),
        in_specs=[pl.BlockSpec((N_EXPERTS, tr), lambda i: (0, i))],
        out_specs=[tok, tok, tok, pl.BlockSpec((N_EXPERTS, LANES), lambda i: (0, 0))],
        out_shape=[
            jax.ShapeDtypeStruct((MOE_TOP_K, T), jnp.int32),
            jax.ShapeDtypeStruct((MOE_TOP_K, T), jnp.float32),
            jax.ShapeDtypeStruct((MOE_TOP_K, T), jnp.int32),
            jax.ShapeDtypeStruct((N_EXPERTS, LANES), jnp.float32),
        ],
        scratch_shapes=[pltpu.VMEM((N_EXPERTS, LANES), jnp.float32)],
        compiler_params=_cparams(("arbitrary",)),
        name="route",
    )(logits_t)


def _dest_body(poff_ref, eidx_ref, rank_ref, dest_ref):
    e = eidx_ref[...]
    base = jnp.zeros(e.shape, jnp.int32)
    for x in range(N_EXPERTS):
        base = jnp.where(e == x, poff_ref[x], base)
    dest_ref[...] = base + rank_ref[...]


def _dest(pad_off, eidx, rank, tr):
    T = eidx.shape[1]
    tok = pl.BlockSpec((MOE_TOP_K, tr), lambda i, p: (0, i))
    return pl.pallas_call(
        _dest_body,
        grid_spec=pltpu.PrefetchScalarGridSpec(
            num_scalar_prefetch=1, grid=(T // tr,), in_specs=[tok, tok], out_specs=tok),
        out_shape=jax.ShapeDtypeStruct((MOE_TOP_K, T), jnp.int32),
        compiler_params=_cparams(("arbitrary",)),
        name="dest",
    )(pad_off, eidx, rank)


def _dispatch_body(zstart_ref, zvalid_ref, dest_ref, hn_ref, xs_ref, z_scr, zsem, sem, *, tm):
    @pl.when(pl.program_id(0) == 0)
    def _():
        z_scr[...] = jnp.zeros_like(z_scr)

        def zero_copy(n):
            return pltpu.make_async_copy(z_scr, xs_ref.at[pl.ds(zstart_ref[n], MOE_BLOCK)], zsem)

        def zissue(n, _):
            @pl.when(zvalid_ref[n] > 0)
            def _():
                zero_copy(n).start()
            return 0

        def zdrain(n, _):
            @pl.when(zvalid_ref[n] > 0)
            def _():
                zero_copy(n).wait()
            return 0

        lax.fori_loop(0, 2 * N_EXPERTS, zissue, 0)
        lax.fori_loop(0, 2 * N_EXPERTS, zdrain, 0)

    def row_copy(t, d):
        return pltpu.make_async_copy(hn_ref.at[t], xs_ref.at[d], sem)

    def issue(t, _):
        for k in range(MOE_TOP_K):
            row_copy(t, dest_ref[k, t]).start()
        return 0

    lax.fori_loop(0, tm, issue, 0)

    def drain(t, _):
        for k in range(MOE_TOP_K):
            row_copy(t, dest_ref[k, t]).wait()
        return 0

    lax.fori_loop(0, tm, drain, 0)


def _dispatch(zero_start, zero_valid, dest, hn, P, tm):
    T = hn.shape[0]
    return pl.pallas_call(
        functools.partial(_dispatch_body, tm=tm),
        grid_spec=pltpu.PrefetchScalarGridSpec(
            num_scalar_prefetch=2, grid=(T // tm,),
            in_specs=[
                pl.BlockSpec((MOE_TOP_K, tm), lambda i, zs, zv: (0, i), memory_space=pltpu.SMEM),
                pl.BlockSpec((tm,) + ROW_TILE, lambda i, zs, zv: (i, 0, 0)),
            ],
            out_specs=pl.BlockSpec(memory_space=pl.ANY),
            scratch_shapes=[pltpu.VMEM((MOE_BLOCK,) + ROW_TILE, hn.dtype), pltpu.SemaphoreType.DMA(()),
                            pltpu.SemaphoreType.DMA(())]),
        out_shape=jax.ShapeDtypeStruct((P,) + ROW_TILE, hn.dtype),
        compiler_params=_cparams(("arbitrary",)),
        name="dispatch",
    )(zero_start, zero_valid, dest, hn)


def _experts_body(bexp_ref, first_ref, nused_ref, xs_ref, wgu_ref, bgu_ref, wd_ref, bd_ref, y_ref, wgu_s, wd_s):
    j = pl.program_id(0)

    @pl.when(first_ref[j] > 0)
    def _():
        wgu_s[...] = wgu_ref[0].astype(jnp.bfloat16)
        wd_s[...] = wd_ref[0].astype(jnp.bfloat16)

    @pl.when(j < nused_ref[0])
    def _():
        xb = xs_ref[...].reshape(MOE_BLOCK, D_MODEL).astype(jnp.bfloat16)
        h = jnp.dot(xb, wgu_s[...], preferred_element_type=jnp.float32) + bgu_ref[0]
        gate = jnp.minimum(h[:, :D_EXPERT], SWIGLU_LIMIT)
        lin = jnp.clip(h[:, D_EXPERT:], -SWIGLU_LIMIT, SWIGLU_LIMIT)
        glu = gate * jax.nn.sigmoid(SWIGLU_ALPHA * gate)
        act = ((lin + 1.0) * glu).astype(jnp.bfloat16)
        y = jnp.dot(act, wd_s[...], preferred_element_type=jnp.float32) + bd_ref[0]
        y_ref[...] = y.reshape(y_ref.shape)

    @pl.when(j >= nused_ref[0])
    def _():
        y_ref[...] = jnp.zeros_like(y_ref)


def _experts(blk_expert, blk_first, n_used, xs, wgu, bgu, wd, bd):
    P = xs.shape[0]
    nb = P // MOE_BLOCK
    rows = pl.BlockSpec((MOE_BLOCK,) + ROW_TILE, lambda j, be, bf, nu: (j, 0, 0))
    return pl.pallas_call(
        _experts_body,
        grid_spec=pltpu.PrefetchScalarGridSpec(
            num_scalar_prefetch=3, grid=(nb,),
            in_specs=[
                rows,
                pl.BlockSpec((1, D_MODEL, 2 * D_EXPERT), lambda j, be, bf, nu: (be[j], 0, 0)),
                pl.BlockSpec((1, 1, 2 * D_EXPERT), lambda j, be, bf, nu: (be[j], 0, 0)),
                pl.BlockSpec((1, D_EXPERT, D_MODEL), lambda j, be, bf, nu: (be[j], 0, 0)),
                pl.BlockSpec((1, 1, D_MODEL), lambda j, be, bf, nu: (be[j], 0, 0)),
            ],
            out_specs=rows,
            scratch_shapes=[pltpu.VMEM((D_MODEL, 2 * D_EXPERT), jnp.bfloat16),
                            pltpu.VMEM((D_EXPERT, D_MODEL), jnp.bfloat16)]),
        out_shape=jax.ShapeDtypeStruct((P,) + ROW_TILE, jnp.float32),
        compiler_params=_cparams(("arbitrary",)),
        name="experts",
    )(blk_expert, blk_first, n_used, xs, wgu, bgu, wd, bd)


def _combine_body(dest_ref, y_ref, x1_ref, gate_ref, gn_ref, o_ref, ybuf, sem, *, tm):
    def row_copy(k, t, d):
        return pltpu.make_async_copy(y_ref.at[d], ybuf.at[k, t], sem)

    def issue(t, _):
        for k in range(MOE_TOP_K):
            row_copy(k, t, dest_ref[k, t]).start()
        return 0

    lax.fori_loop(0, tm, issue, 0)

    def drain(t, _):
        for k in range(MOE_TOP_K):
            row_copy(k, t, dest_ref[k, t]).wait()
        return 0

    lax.fori_loop(0, tm, drain, 0)

    g = gate_ref[...]
    acc = x1_ref[...]
    for k in range(MOE_TOP_K):
        acc = acc + g[:, k:k + 1] * ybuf[k].reshape(tm, D_MODEL)
    ms = jnp.mean(acc * acc, axis=-1, keepdims=True)
    o_ref[...] = acc * lax.rsqrt(ms + NORM_EPS) * gn_ref[...]


def _combine(dest, y, x1, gate_t, gn, tm):
    T = x1.shape[0]
    return pl.pallas_call(
        functools.partial(_combine_body, tm=tm),
        grid=(T // tm,),
        in_specs=[
            pl.BlockSpec((MOE_TOP_K, tm), lambda i: (0, i), memory_space=pltpu.SMEM),
            pl.BlockSpec(memory_space=pl.ANY),
            pl.BlockSpec((tm, D_MODEL), lambda i: (i, 0)),
            pl.BlockSpec((tm, MOE_TOP_K), lambda i: (i, 0)),
            pl.BlockSpec((1, D_MODEL), lambda i: (0, 0)),
        ],
        out_specs=pl.BlockSpec((tm, D_MODEL), lambda i: (i, 0)),
        out_shape=jax.ShapeDtypeStruct((T, D_MODEL), jnp.float32),
        scratch_shapes=[pltpu.VMEM((MOE_TOP_K, tm) + ROW_TILE, jnp.float32), pltpu.SemaphoreType.DMA(())],
        compiler_params=_cparams(("arbitrary",)),
        name="combine",
    )(dest, y, x1, gate_t, gn)


def _rope_tables(S):
    pos = jnp.arange(S, dtype=jnp.float32)[:, None]

    def cs(half):
        inv = ROPE_THETA ** (-jnp.arange(half, dtype=jnp.float32) / half)
        ang = pos * inv[None, :]
        return jnp.cos(ang), jnp.sin(ang)

    z = lambda n: jnp.zeros((S, n), jnp.float32)
    o = lambda n: jnp.ones((S, n), jnp.float32)
    cat = lambda *a: jnp.concatenate(a, axis=1)
    c, s = cs(MLA_ROPE // 2)
    mla = (cat(c, c, z(64)), cat(z(32), s, z(64)), cat(-s, z(96)))
    c, s = cs(DSA_HEAD_DIM // 2)
    dsa = (cat(c, c), cat(-s, s))
    c, s = cs(IDX_ROPE // 2)
    c1, sa1, sb1 = cat(c, c, o(32)), cat(z(16), s, z(32)), cat(-s, z(48))
    idx_q = (cat(c1, c1), cat(sa1, sa1), cat(sb1, sb1))
    idx_k = (cat(c1, o(64)), cat(sa1, z(64)), cat(sb1, z(64)))
    return mla, dsa + idx_q + idx_k


def _pack_w_in(w):
    sizes = (MLA_Q_LORA, MLA_KV_LORA, MLA_ROPE, 3 * DSA_HEADS * DSA_HEAD_DIM, IDX_HEADS * IDX_HEAD_DIM,
             IDX_HEAD_DIM, IDX_HEADS, 2 * D_MODEL)
    pts, acc = [], 0
    for s in sizes[:-1]:
        acc += s
        pts.append(acc)
    c_q, c_kv, k_r, qkv, q_idx, k_idx, w_idx, gates = jnp.split(w, pts, axis=1)
    z = lambda n: jnp.zeros((w.shape[0], n), w.dtype)
    packed = jnp.concatenate([gates, qkv, q_idx, k_idx, w_idx, z(LANES - IDX_HEAD_DIM - IDX_HEADS),
                              c_q, c_kv, k_r, z(LANES - MLA_ROPE)], axis=1)
    assert packed.shape[1] == D_IN_PACKED
    return packed.astype(jnp.bfloat16)


def _pack_w_uq(w):
    w = w.reshape(MLA_Q_LORA, MLA_HEADS, MLA_QK)
    w = jnp.pad(w, ((0, 0), (0, 0), (0, MLA_QK_PAD - MLA_QK)))
    return w.reshape(MLA_Q_LORA, MLA_HEADS * MLA_QK_PAD).astype(jnp.bfloat16)


def _tile(n, pref):
    t = min(n, pref)
    assert n % t == 0
    return t


def kernel(x, norm_attn_g, w_in, q_lora_g, kv_lora_g, w_uq, w_uk, w_uv, w_o_mla, w_o_dsa, b_gate, w_out,
           norm_ffn_g, w_router, b_router, w_gate_up, b_gate_up, w_down, b_down, norm_final_g):
    B, S, D = x.shape
    assert D == D_MODEL and S % CHUNK == 0 and norm_attn_g.shape[0] == 1
    T = B * S
    top_k = min(DSA_TOPK_MAX, S // 4)
    bf = jnp.bfloat16
    x2 = x.reshape(T, D)
    mla_tabs, dsa_tabs = _rope_tables(S)

    proj = _in_proj(x2, norm_attn_g, _pack_w_in(w_in[0]), _tile(T, 1024), D_IN_PACKED // 3)
    tm = _tile(S, 512)
    q, k, v = _mla_prep(proj, q_lora_g, kv_lora_g, _pack_w_uq(w_uq[0]), w_uk[0].astype(bf), w_uv[0].astype(bf),
                        mla_tabs, S, tm)
    o_mla = _mla_attn(q, k, v, B, S, _tile(S, 512))
    qd, kd, qi, kia, kib, wi = _dsa_prep(proj, dsa_tabs, S, tm)
    o_dsa = _dsa_attn(qd, qi, wi, kd, proj, kia, kib, B, S, _tile(S, 256), _tile(S, 512), top_k)
    x1, hn, logits_t = _merge(o_mla, o_dsa, proj, x2, w_o_mla[0].astype(bf), w_o_dsa[0].astype(bf),
                              w_out[0].astype(bf), b_gate, norm_ffn_g, w_router[0].T.astype(bf),
                              b_router[0][:, None], _tile(T, 512))

    tr = _tile(T, 512)
    eidx, gate, rank, cnt = _route(logits_t, tr)
    counts = cnt[:, 0].astype(jnp.int32)
    padded = ((counts + MOE_BLOCK - 1) // MOE_BLOCK) * MOE_BLOCK
    pad_end = jnp.cumsum(padded)
    pad_off = pad_end - padded
    P = T * MOE_TOP_K + N_EXPERTS * MOE_BLOCK
    nb = P // MOE_BLOCK
    blk_start = jnp.arange(nb, dtype=jnp.int32) * MOE_BLOCK
    blk_expert = jnp.minimum(jnp.sum(pad_end[None, :] <= blk_start[:, None], axis=1), N_EXPERTS - 1).astype(jnp.int32)
    blk_first = jnp.concatenate([jnp.ones((1,), jnp.int32), (blk_expert[1:] != blk_expert[:-1]).astype(jnp.int32)])
    n_used = (pad_end[-1:] // MOE_BLOCK).astype(jnp.int32)

    dest = _dest(pad_off.astype(jnp.int32), eidx, rank, tr)
    trailing = n_used[0] + jnp.arange(N_EXPERTS, dtype=jnp.int32)
    zero_start = jnp.concatenate([pad_end - MOE_BLOCK, jnp.minimum(trailing, nb - 1) * MOE_BLOCK]).astype(jnp.int32)
    zero_valid = jnp.concatenate([padded > 0, trailing < nb]).astype(jnp.int32)
    xs = _dispatch(zero_start, zero_valid, dest, hn, P, _tile(T, 128))
    y = _experts(blk_expert, blk_first, n_used, xs, w_gate_up[0], b_gate_up[0][:, None, :],
                 w_down[0], b_down[0][:, None, :])
    out = _combine(dest, y, x1, gate.T, norm_final_g[None, :], _tile(T, 128))
    return out.reshape(B, S, D)
```

```python
import functools

import jax
import jax.numpy as jnp
from jax import lax
from jax.experimental import pallas as pl
from jax.experimental.pallas import tpu as pltpu

D_MODEL = 1024
CHUNK = 64
ROPE_THETA = 10000.0
NORM_EPS = 1e-6

MLA_HEADS = 8
MLA_Q_LORA = 384
MLA_KV_LORA = 256
MLA_NOPE = 128
MLA_ROPE = 64
MLA_V = 128
MLA_QK = MLA_NOPE + MLA_ROPE
MLA_QK_PAD = 256

DSA_HEADS = 8
DSA_HEAD_DIM = 128
IDX_HEADS = 8
IDX_HEAD_DIM = 64
IDX_ROPE = 32
DSA_TOPK_MAX = 256

N_EXPERTS = 32
MOE_TOP_K = 4
D_EXPERT = 1024
SWIGLU_LIMIT = 7.0
SWIGLU_ALPHA = 1.702

LANES = 128
MOE_BLOCK = 512
ROW_TILE = (D_MODEL // LANES, LANES)
VMEM_LIMIT = 56 * 1024 * 1024

OFF_GATES = 0
OFF_Q = 2048
OFF_K = 3072
OFF_V = 4096
OFF_QIDX = 5120
OFF_KIDX = 5632
OFF_CQ = 5760
OFF_CKV = 6144
OFF_KR = 6400
D_IN_PACKED = 6528

NEG = -1e30
INT_MIN = -(2 ** 31)
KEY_NEG_INF = -2139095041

_NT = (((1,), (1,)), ((), ()))


def _cparams(sem):
    return pltpu.CompilerParams(dimension_semantics=sem, vmem_limit_bytes=VMEM_LIMIT)


def _rope3(x, c, sa, sb, shift):
    n = x.shape[-1]
    return x * c + pltpu.roll(x, shift, 1) * sa + pltpu.roll(x, n - shift, 1) * sb


def _inproj_body(x_ref, g_ref, w_ref, o_ref, h_scr):
    @pl.when(pl.program_id(1) == 0)
    def _():
        x = x_ref[...]
        ms = jnp.mean(x * x, axis=-1, keepdims=True)
        h_scr[...] = (x * lax.rsqrt(ms + NORM_EPS) * g_ref[...]).astype(jnp.bfloat16)

    o_ref[...] = jnp.dot(h_scr[...], w_ref[...], preferred_element_type=jnp.float32).astype(o_ref.dtype)


def _in_proj(x2, g, w_packed, tm, tn):
    T = x2.shape[0]
    return pl.pallas_call(
        _inproj_body,
        grid=(T // tm, D_IN_PACKED // tn),
        in_specs=[
            pl.BlockSpec((tm, D_MODEL), lambda i, j: (i, 0)),
            pl.BlockSpec((1, D_MODEL), lambda i, j: (0, 0)),
            pl.BlockSpec((D_MODEL, tn), lambda i, j: (0, j)),
        ],
        out_specs=pl.BlockSpec((tm, tn), lambda i, j: (i, j)),
        out_shape=jax.ShapeDtypeStruct((T, D_IN_PACKED), jnp.bfloat16),
        scratch_shapes=[pltpu.VMEM((tm, D_MODEL), jnp.bfloat16)],
        compiler_params=_cparams(("arbitrary", "arbitrary")),
        name="in_proj",
    )(x2, g, w_packed)


def _mla_prep_body(cq_ref, ckv_ref, kr_ref, gq_ref, gkv_ref, wq_ref, wk_ref, wv_ref,
                   c_ref, sa_ref, sb_ref, q_ref, k_ref, v_ref):
    def norm(ref, g_ref):
        x = ref[...].astype(jnp.float32)
        ms = jnp.mean(x * x, axis=-1, keepdims=True)
        return (x * lax.rsqrt(ms + NORM_EPS) * g_ref[...]).astype(jnp.bfloat16)

    cq = norm(cq_ref, gq_ref)
    ckv = norm(ckv_ref, gkv_ref)
    c, sa, sb = c_ref[...], sa_ref[...], sb_ref[...]
    scale = MLA_QK ** -0.5
    q_all = jnp.dot(cq, wq_ref[...], preferred_element_type=jnp.float32)
    kn = jnp.dot(ckv, wk_ref[...], preferred_element_type=jnp.float32)
    v_ref[0] = lax.dot_general(wv_ref[...], ckv, _NT, preferred_element_type=jnp.float32).astype(v_ref.dtype)
    kr = _rope3(kr_ref[...].astype(jnp.float32), c, sa, sb, MLA_ROPE // 2).astype(k_ref.dtype)
    for h in range(MLA_HEADS):
        o = h * MLA_QK_PAD
        q_ref[:, o:o + LANES] = (q_all[:, o:o + LANES] * scale).astype(q_ref.dtype)
        qr = _rope3(q_all[:, o + LANES:o + 2 * LANES], c, sa, sb, MLA_ROPE // 2)
        q_ref[:, o + LANES:o + 2 * LANES] = (qr * scale).astype(q_ref.dtype)
        k_ref[:, o:o + LANES] = kn[:, h * LANES:(h + 1) * LANES].astype(k_ref.dtype)
        k_ref[:, o + LANES:o + 2 * LANES] = kr


def _mla_prep(proj, gq, gkv, wq, wk, wv, tabs, S, tm):
    T = proj.shape[0]
    ns = S // tm
    full = lambda shape: pl.BlockSpec(shape, lambda i: (0, 0))
    tab = pl.BlockSpec((tm, LANES), lambda i: (i % ns, 0))
    hq = MLA_HEADS * MLA_QK_PAD
    return pl.pallas_call(
        _mla_prep_body,
        grid=(T // tm,),
        in_specs=[
            pl.BlockSpec((tm, MLA_Q_LORA), lambda i: (i, OFF_CQ // MLA_Q_LORA)),
            pl.BlockSpec((tm, MLA_KV_LORA), lambda i: (i, OFF_CKV // MLA_KV_LORA)),
            pl.BlockSpec((tm, LANES), lambda i: (i, OFF_KR // LANES)),
            full((1, MLA_Q_LORA)), full((1, MLA_KV_LORA)),
            full((MLA_Q_LORA, hq)), full((MLA_KV_LORA, MLA_HEADS * MLA_NOPE)),
            full((MLA_HEADS * MLA_V, MLA_KV_LORA)),
            tab, tab, tab,
        ],
        out_specs=[
            pl.BlockSpec((tm, hq), lambda i: (i, 0)),
            pl.BlockSpec((tm, hq), lambda i: (i, 0)),
            pl.BlockSpec((1, MLA_HEADS * MLA_V, tm), lambda i: (i, 0, 0)),
        ],
        out_shape=[
            jax.ShapeDtypeStruct((T, hq), jnp.bfloat16),
            jax.ShapeDtypeStruct((T, hq), jnp.bfloat16),
            jax.ShapeDtypeStruct((T // tm, MLA_HEADS * MLA_V, tm), jnp.bfloat16),
        ],
        compiler_params=_cparams(("arbitrary",)),
        name="mla_prep",
    )(proj, proj, proj, gq, gkv, wq, wk, wv, *tabs)


def _flash_t_init(tq, dv):
    return (jnp.full((1, tq), NEG, jnp.float32), jnp.zeros((1, tq), jnp.float32), jnp.zeros((dv, tq), jnp.float32))


def _flash_t_step(carry, qh, kb, vtb, bias):
    m, l, acc = carry
    s = lax.dot_general(kb, qh, _NT, preferred_element_type=jnp.float32)
    if bias is not None:
        s = s + bias
    m_new = jnp.maximum(m, jnp.max(s, axis=0, keepdims=True))
    alpha = jnp.exp(m - m_new)
    p = jnp.exp(s - m_new)
    l = alpha * l + jnp.sum(p, axis=0, keepdims=True)
    acc = alpha * acc + jnp.dot(vtb, p.astype(vtb.dtype), preferred_element_type=jnp.float32)
    return m_new, l, acc


def _mla_attn_body(q_ref, k_ref, vt_ref, o_ref, *, tq, tk):
    i = pl.program_id(1)
    q0 = i * tq
    nkv = (q0 + tq + tk - 1) // tk
    last = nkv - 1
    kc = (last * tk + lax.broadcasted_iota(jnp.int32, (tk, tq), 0)) // CHUNK
    qc = (q0 + lax.broadcasted_iota(jnp.int32, (tk, tq), 1)) // CHUNK
    last_bias = jnp.where(kc <= qc, 0.0, NEG)

    for h0 in range(0, MLA_HEADS, 2):
        heads = (h0, h0 + 1)
        qs = [q_ref[:, h * MLA_QK_PAD:(h + 1) * MLA_QK_PAD] for h in heads]

        def step(j, carries, bias, heads=heads, qs=qs):
            off = pl.multiple_of(j * tk, tk)
            out = []
            for h, qh, c in zip(heads, qs, carries):
                kb = k_ref[pl.ds(off, tk), h * MLA_QK_PAD:(h + 1) * MLA_QK_PAD]
                vtb = vt_ref[j, h * MLA_V:(h + 1) * MLA_V, :]
                out.append(_flash_t_step(c, qh, kb, vtb, bias))
            return tuple(out)

        init = tuple(_flash_t_init(tq, MLA_V) for _ in heads)
        carries = lax.fori_loop(0, last, lambda j, c: step(j, c, None), init)
        carries = step(last, carries, last_bias)
        for h, (_, l, acc) in zip(heads, carries):
            o_ref[:, h * MLA_V:(h + 1) * MLA_V] = (acc / l).T.astype(o_ref.dtype)


def _mla_attn(q, k, vt, B, S, tq, tk):
    T = B * S
    nq = S // tq
    hq = MLA_HEADS * MLA_QK_PAD
    hv = MLA_HEADS * MLA_V
    assert vt.shape == (T // tk, hv, tk)
    return pl.pallas_call(
        functools.partial(_mla_attn_body, tq=tq, tk=tk),
        grid=(B, nq),
        in_specs=[
            pl.BlockSpec((tq, hq), lambda b, i: (b * nq + i, 0)),
            pl.BlockSpec((S, hq), lambda b, i: (b, 0)),
            pl.BlockSpec((S // tk, hv, tk), lambda b, i: (b, 0, 0)),
        ],
        out_specs=pl.BlockSpec((tq, hv), lambda b, i: (b * nq + i, 0)),
        out_shape=jax.ShapeDtypeStruct((T, hv), jnp.bfloat16),
        compiler_params=_cparams(("arbitrary", "arbitrary")),
        name="mla_attn",
    )(q, k, vt)


def _dsa_prep_body(q_ref, k_ref, v_ref, qi_ref, kx_ref, cd_ref, sd_ref, ci_ref, sai_ref, sbi_ref,
                   ck_ref, sak_ref, sbk_ref, qo_ref, ko_ref, vt_ref, qio_ref, kia_ref, kib_ref, wit_ref):
    cd, sd = cd_ref[...], sd_ref[...]
    for h in range(DSA_HEADS):
        sl = slice(h * LANES, (h + 1) * LANES)
        xq = q_ref[:, sl].astype(jnp.float32)
        xk = k_ref[:, sl].astype(jnp.float32)
        rq = xq * cd + pltpu.roll(xq, DSA_HEAD_DIM // 2, 1) * sd
        rk = xk * cd + pltpu.roll(xk, DSA_HEAD_DIM // 2, 1) * sd
        qo_ref[:, sl] = (rq * (DSA_HEAD_DIM ** -0.5)).astype(qo_ref.dtype)
        ko_ref[:, sl] = rk.astype(ko_ref.dtype)
        vt_ref[0, sl, :] = v_ref[:, sl].astype(jnp.float32).T.astype(vt_ref.dtype)
    ci, sai, sbi = ci_ref[...], sai_ref[...], sbi_ref[...]
    for p in range(IDX_HEADS * IDX_HEAD_DIM // LANES):
        sl = slice(p * LANES, (p + 1) * LANES)
        r = _rope3(qi_ref[:, sl].astype(jnp.float32), ci, sai, sbi, IDX_ROPE // 2)
        qio_ref[:, sl] = (r * (IDX_HEAD_DIM ** -0.5)).astype(qio_ref.dtype)
    kx = kx_ref[...].astype(jnp.float32)
    kr = _rope3(kx, ck_ref[...], sak_ref[...], sbk_ref[...], IDX_ROPE // 2)
    lane = lax.broadcasted_iota(jnp.int32, kr.shape, 1)
    ka = jnp.where(lane < IDX_HEAD_DIM, kr, 0.0)
    kia_ref[...] = ka.astype(kia_ref.dtype)
    kib_ref[...] = pltpu.roll(ka, IDX_HEAD_DIM, 1).astype(kib_ref.dtype)
    wit_ref[...] = kx.T[IDX_HEAD_DIM:IDX_HEAD_DIM + IDX_HEADS, :] * (IDX_HEADS ** -0.5)


def _dsa_prep(proj, tabs, S, tm):
    T = proj.shape[0]
    ns = S // tm
    tab = pl.BlockSpec((tm, LANES), lambda i: (i % ns, 0))
    hd = DSA_HEADS * DSA_HEAD_DIM
    hi = IDX_HEADS * IDX_HEAD_DIM
    row = lambda w, off: pl.BlockSpec((tm, w), lambda i: (i, off // w))
    out = lambda w: pl.BlockSpec((tm, w), lambda i: (i, 0))
    return pl.pallas_call(
        _dsa_prep_body,
        grid=(T // tm,),
        in_specs=[row(hd, OFF_Q), row(hd, OFF_K), row(hd, OFF_V), row(hi, OFF_QIDX), row(LANES, OFF_KIDX)] + [tab] * 8,
        out_specs=[out(hd), out(hd), pl.BlockSpec((1, hd, tm), lambda i: (i, 0, 0)), out(hi), out(LANES), out(LANES),
                   pl.BlockSpec((IDX_HEADS, tm), lambda i: (0, i))],
        out_shape=[
            jax.ShapeDtypeStruct((T, hd), jnp.bfloat16),
            jax.ShapeDtypeStruct((T, hd), jnp.bfloat16),
            jax.ShapeDtypeStruct((T // tm, hd, tm), jnp.bfloat16),
            jax.ShapeDtypeStruct((T, hi), jnp.bfloat16),
            jax.ShapeDtypeStruct((T, LANES), jnp.bfloat16),
            jax.ShapeDtypeStruct((T, LANES), jnp.bfloat16),
            jax.ShapeDtypeStruct((IDX_HEADS, T), jnp.float32),
        ],
        compiler_params=_cparams(("arbitrary",)),
        name="dsa_prep",
    )(proj, proj, proj, proj, proj, *tabs)


def _dsa_attn_body(qd_ref, qi_ref, wit_ref, kd_ref, vt_ref, kia_ref, kib_ref, o_ref, key_scr, bias_scr,
                   *, tq, tk, top_k, idx_bits, seq_len):
    i = pl.program_id(1)
    q0 = i * tq
    nkv = (q0 + tq + tk - 1) // tk
    qchunk = (q0 + lax.broadcasted_iota(jnp.int32, (1, tq), 1)) // CHUNK
    row_k = lax.broadcasted_iota(jnp.int32, (tk, tq), 0)
    wit = wit_ref[...]

    def idx_block(j, _):
        off = pl.multiple_of(j * tk, tk)
        ka = kia_ref[pl.ds(off, tk), :]
        kb = kib_ref[pl.ds(off, tk), :]
        score = jnp.zeros((tk, tq), jnp.float32)
        for p in range(IDX_HEADS // 2):
            qp = qi_ref[:, p * LANES:(p + 1) * LANES]
            ra = lax.dot_general(ka, qp, _NT, preferred_element_type=jnp.float32)
            rb = lax.dot_general(kb, qp, _NT, preferred_element_type=jnp.float32)
            score = score + wit[2 * p:2 * p + 1, :] * jnp.maximum(ra, 0.0)
            score = score + wit[2 * p + 1:2 * p + 2, :] * jnp.maximum(rb, 0.0)
        adm = ((off + row_k) // CHUNK) <= qchunk
        score = jnp.where(adm, score + 0.0, -jnp.inf)
        b = pltpu.bitcast(score, jnp.int32)
        key_scr[j] = b ^ ((b >> 31) & 0x7FFFFFFF)
        return 0

    lax.fori_loop(0, nkv, idx_block, 0)

    def count(pred):
        def blk(j, part):
            c = pred(key_scr[j], j * tk + row_k).astype(jnp.int32)
            return part + jnp.sum(c, axis=0, keepdims=True)
        return lax.fori_loop(0, nkv, blk, jnp.zeros((1, tq), jnp.int32))

    tau = jnp.where(count(lambda k, _: k >= 0) >= top_k, 0, INT_MIN).astype(jnp.int32)

    def tau_bit(it, tau):
        cand = tau | (jnp.int32(1) << (30 - it))
        return jnp.where(count(lambda k, _: k >= cand) >= top_k, cand, tau)

    tau = lax.fori_loop(0, 31, tau_bit, tau)
    n_gt = count(lambda k, _: k > tau)
    n_ge = count(lambda k, _: k >= tau)
    need = top_k - n_gt

    def tie_search():
        def tie_bit(it, mi):
            step = jnp.int32(1) << (idx_bits - 1 - it)
            c = count(lambda k, idx: (k == tau) & (idx <= mi + step - 1))
            return jnp.where(c < need, mi + step, mi)
        return lax.fori_loop(0, idx_bits, tie_bit, jnp.zeros((1, tq), jnp.int32))

    partial_ties = (n_ge > top_k) & (tau > KEY_NEG_INF)
    mi = lax.cond(jnp.max(partial_ties.astype(jnp.int32)) > 0, tie_search,
                  lambda: jnp.full((1, tq), seq_len, jnp.int32))

    def bias_block(j, _):
        k = key_scr[j]
        idx = j * tk + row_k
        sel = (k > tau) | ((k == tau) & (idx <= mi))
        adm = (idx // CHUNK) <= qchunk
        bias_scr[j] = jnp.where(sel & adm, 0.0, NEG)
        return 0

    lax.fori_loop(0, nkv, bias_block, 0)

    for h0 in range(0, DSA_HEADS, 2):
        heads = (h0, h0 + 1)
        qs = [qd_ref[:, h * DSA_HEAD_DIM:(h + 1) * DSA_HEAD_DIM] for h in heads]

        def att(j, carries, heads=heads, qs=qs):
            off = pl.multiple_of(j * tk, tk)
            bias = bias_scr[j]
            out = []
            for h, qh, c in zip(heads, qs, carries):
                sl = slice(h * DSA_HEAD_DIM, (h + 1) * DSA_HEAD_DIM)
                out.append(_flash_t_step(c, qh, kd_ref[pl.ds(off, tk), sl], vt_ref[j, sl, :], bias))
            return tuple(out)

        carries = lax.fori_loop(0, nkv, att, tuple(_flash_t_init(tq, DSA_HEAD_DIM) for _ in heads))
        for h, (_, l, acc) in zip(heads, carries):
            o_ref[:, h * DSA_HEAD_DIM:(h + 1) * DSA_HEAD_DIM] = (acc / l).T.astype(o_ref.dtype)


def _dsa_attn(qd, qi, wit, kd, vt, kia, kib, B, S, tq, tk, top_k):
    T = B * S
    nq = S // tq
    hd = DSA_HEADS * DSA_HEAD_DIM
    hi = IDX_HEADS * IDX_HEAD_DIM
    idx_bits = max(1, (S - 1).bit_length())
    assert vt.shape == (T // tk, hd, tk)
    qrow = lambda w: pl.BlockSpec((tq, w), lambda b, i: (b * nq + i, 0))
    seq = lambda w: pl.BlockSpec((S, w), lambda b, i: (b, 0))
    return pl.pallas_call(
        functools.partial(_dsa_attn_body, tq=tq, tk=tk, top_k=top_k, idx_bits=idx_bits, seq_len=S),
        grid=(B, nq),
        in_specs=[qrow(hd), qrow(hi), pl.BlockSpec((IDX_HEADS, tq), lambda b, i: (0, b * nq + i)), seq(hd),
                  pl.BlockSpec((S // tk, hd, tk), lambda b, i: (b, 0, 0)), seq(LANES), seq(LANES)],
        out_specs=qrow(hd),
        out_shape=jax.ShapeDtypeStruct((T, hd), jnp.bfloat16),
        scratch_shapes=[pltpu.VMEM((S // tk, tk, tq), jnp.int32), pltpu.VMEM((S // tk, tk, tq), jnp.float32)],
        compiler_params=_cparams(("arbitrary", "arbitrary")),
        name="dsa_attn",
    )(qd, qi, wit, kd, vt, kia, kib)


def _merge_body(oa_ref, ob_ref, gc_ref, x_ref, woa_ref, wob_ref, wout_ref, bg_ref, gf_ref, wr_ref, br_ref,
                x1_ref, hn_ref, lg_ref):
    ya = jnp.dot(oa_ref[...], woa_ref[...], preferred_element_type=jnp.float32)
    yb = jnp.dot(ob_ref[...], wob_ref[...], preferred_element_type=jnp.float32)
    g = jax.nn.sigmoid(gc_ref[...].astype(jnp.float32) + bg_ref[...])
    merged = g[:, :D_MODEL] * ya + g[:, D_MODEL:] * yb
    x1 = x_ref[...] + jnp.dot(merged.astype(jnp.bfloat16), wout_ref[...], preferred_element_type=jnp.float32)
    x1_ref[...] = x1
    ms = jnp.mean(x1 * x1, axis=-1, keepdims=True)
    hn = x1 * lax.rsqrt(ms + NORM_EPS) * gf_ref[...]
    hn_ref[...] = hn.reshape(hn_ref.shape)
    lg_ref[...] = lax.dot_general(wr_ref[...], hn.astype(jnp.bfloat16), _NT,
                                  preferred_element_type=jnp.float32) + br_ref[...]


def _merge(oa, ob, proj, x2, woa, wob, wout, bg, gf, wr_t, br, tm):
    T = x2.shape[0]
    row = lambda w: pl.BlockSpec((tm, w), lambda i: (i, 0))
    full = lambda shape: pl.BlockSpec(shape, lambda i: (0, 0))
    return pl.pallas_call(
        _merge_body,
        grid=(T // tm,),
        in_specs=[row(D_MODEL), row(D_MODEL), row(2 * D_MODEL), row(D_MODEL),
                  full((D_MODEL, D_MODEL)), full((D_MODEL, D_MODEL)), full((D_MODEL, D_MODEL)),
                  full((1, 2 * D_MODEL)), full((1, D_MODEL)), full((N_EXPERTS, D_MODEL)), full((N_EXPERTS, 1))],
        out_specs=[row(D_MODEL), pl.BlockSpec((tm,) + ROW_TILE, lambda i: (i, 0, 0)),
                   pl.BlockSpec((N_EXPERTS, tm), lambda i: (0, i))],
        out_shape=[
            jax.ShapeDtypeStruct((T, D_MODEL), jnp.float32),
            jax.ShapeDtypeStruct((T,) + ROW_TILE, jnp.float32),
            jax.ShapeDtypeStruct((N_EXPERTS, T), jnp.float32),
        ],
        compiler_params=_cparams(("arbitrary",)),
        name="merge",
    )(oa, ob, proj, x2, woa, wob, wout, bg, gf, wr_t, br)


def _route_body(lg_ref, eidx_ref, gate_ref, rank_ref, cnt_ref, carry_scr, *, tr):
    @pl.when(pl.program_id(0) == 0)
    def _():
        carry_scr[...] = jnp.zeros_like(carry_scr)

    vals = lg_ref[...]
    rows = lax.broadcasted_iota(jnp.int32, vals.shape, 0)
    tops, hots = [], []
    for k in range(MOE_TOP_K):
        m = jnp.max(vals, axis=0, keepdims=True)
        idx = jnp.min(jnp.where(vals == m, rows, N_EXPERTS), axis=0, keepdims=True)
        hot = rows == idx
        vals = jnp.where(hot, -jnp.inf, vals)
        tops.append(m)
        hots.append(hot)
        eidx_ref[k:k + 1, :] = idx
    es = [jnp.exp(m - tops[0]) for m in tops]
    den = es[0] + es[1] + es[2] + es[3]
    for k in range(MOE_TOP_K):
        gate_ref[k:k + 1, :] = es[k] / den
    sel = jnp.zeros(vals.shape, jnp.float32)
    for hot in hots:
        sel = sel + hot.astype(jnp.float32)
    before = (lax.broadcasted_iota(jnp.int32, (tr, tr), 0) < lax.broadcasted_iota(jnp.int32, (tr, tr), 1))
    prefix = jnp.dot(sel.astype(jnp.bfloat16), before.astype(jnp.bfloat16), preferred_element_type=jnp.float32)
    prefix = prefix + carry_scr[:, 0:1]
    for k in range(MOE_TOP_K):
        rank_ref[k:k + 1, :] = jnp.sum(jnp.where(hots[k], prefix, 0.0), axis=0, keepdims=True).astype(jnp.int32)
    carry_scr[...] = carry_scr[...] + jnp.sum(sel, axis=1, keepdims=True)
    cnt_ref[...] = carry_scr[...]


def _route(logits_t, tr):
    T = logits_t.shape[1]
    tok = pl.BlockSpec((MOE_TOP_K, tr), lambda i: (0, i))
    return pl.pallas_call(
        functools.partial(_route_body, tr=tr),
        grid=(T // tr,),
        in_specs=[pl.BlockSpec((N_EXPERTS, tr), lambda i: (0, i))],
        out_specs=[tok, tok, tok, pl.BlockSpec((N_EXPERTS, LANES), lambda i: (0, 0))],
        out_shape=[
            jax.ShapeDtypeStruct((MOE_TOP_K, T), jnp.int32),
            jax.ShapeDtypeStruct((MOE_TOP_K, T), jnp.float32),
            jax.ShapeDtypeStruct((MOE_TOP_K, T), jnp.int32),
            jax.ShapeDtypeStruct((N_EXPERTS, LANES), jnp.float32),
        ],
        scratch_shapes=[pltpu.VMEM((N_EXPERTS, LANES), jnp.float32)],
        compiler_params=_cparams(("arbitrary",)),
        name="route",
    )(logits_t)


def _dest_body(poff_ref, eidx_ref, rank_ref, dest_ref):
    e = eidx_ref[...]
    base = jnp.zeros(e.shape, jnp.int32)
    for x in range(N_EXPERTS):
        base = jnp.where(e == x, poff_ref[x], base)
    dest_ref[...] = base + rank_ref[...]


def _dest(pad_off, eidx, rank, tr):
    T = eidx.shape[1]
    tok = pl.BlockSpec((MOE_TOP_K, tr), lambda i, p: (0, i))
    return pl.pallas_call(
        _dest_body,
        grid_spec=pltpu.PrefetchScalarGridSpec(
            num_scalar_prefetch=1, grid=(T // tr,), in_specs=[tok, tok], out_specs=tok),
        out_shape=jax.ShapeDtypeStruct((MOE_TOP_K, T), jnp.int32),
        compiler_params=_cparams(("arbitrary",)),
        name="dest",
    )(pad_off, eidx, rank)


def _dispatch_body(zstart_ref, zvalid_ref, dest_ref, hn_ref, xs_ref, z_scr, zsem, sem, *, tm):
    @pl.when(pl.program_id(0) == 0)
    def _():
        z_scr[...] = jnp.zeros_like(z_scr)

        def zero_copy(n):
            return pltpu.make_async_copy(z_scr, xs_ref.at[pl.ds(zstart_ref[n], MOE_BLOCK)], zsem)

        def zissue(n, _):
            @pl.when(zvalid_ref[n] > 0)
            def _():
                zero_copy(n).start()
            return 0

        def zdrain(n, _):
            @pl.when(zvalid_ref[n] > 0)
            def _():
                zero_copy(n).wait()
            return 0

        lax.fori_loop(0, 2 * N_EXPERTS, zissue, 0)
        lax.fori_loop(0, 2 * N_EXPERTS, zdrain, 0)

    def row_copy(t, d):
        return pltpu.make_async_copy(hn_ref.at[t], xs_ref.at[d], sem)

    def issue(t, _):
        for k in range(MOE_TOP_K):
            row_copy(t, dest_ref[k, t]).start()
        return 0

    lax.fori_loop(0, tm, issue, 0)

    def drain(t, _):
        for k in range(MOE_TOP_K):
            row_copy(t, dest_ref[k, t]).wait()
        return 0

    lax.fori_loop(0, tm, drain, 0)


def _dispatch(zero_start, zero_valid, dest, hn, P, tm):
    T = hn.shape[0]
    return pl.pallas_call(
        functools.partial(_dispatch_body, tm=tm),
        grid_spec=pltpu.PrefetchScalarGridSpec(
            num_scalar_prefetch=2, grid=(T // tm,),
            in_specs=[
                pl.BlockSpec((MOE_TOP_K, tm), lambda i, zs, zv: (0, i), memory_space=pltpu.SMEM),
                pl.BlockSpec((tm,) + ROW_TILE, lambda i, zs, zv: (i, 0, 0)),
            ],
            out_specs=pl.BlockSpec(memory_space=pl.ANY),
            scratch_shapes=[pltpu.VMEM((MOE_BLOCK,) + ROW_TILE, hn.dtype), pltpu.SemaphoreType.DMA(()),
                            pltpu.SemaphoreType.DMA(())]),
        out_shape=jax.ShapeDtypeStruct((P,) + ROW_TILE, hn.dtype),
        compiler_params=_cparams(("arbitrary",)),
        name="dispatch",
    )(zero_start, zero_valid, dest, hn)


def _experts_body(bexp_ref, first_ref, nused_ref, xs_ref, wgu_ref, bgu_ref, wd_ref, bd_ref, y_ref, wgu_s, wd_s):
    j = pl.program_id(0)

    @pl.when(first_ref[j] > 0)
    def _():
        wgu_s[...] = wgu_ref[0].astype(jnp.bfloat16)
        wd_s[...] = wd_ref[0].astype(jnp.bfloat16)

    @pl.when(j < nused_ref[0])
    def _():
        xb = xs_ref[...].reshape(MOE_BLOCK, D_MODEL).astype(jnp.bfloat16)
        h = jnp.dot(xb, wgu_s[...], preferred_element_type=jnp.float32) + bgu_ref[0]
        gate = jnp.minimum(h[:, :D_EXPERT], SWIGLU_LIMIT)
        lin = jnp.clip(h[:, D_EXPERT:], -SWIGLU_LIMIT, SWIGLU_LIMIT)
        glu = gate * jax.nn.sigmoid(SWIGLU_ALPHA * gate)
        act = ((lin + 1.0) * glu).astype(jnp.bfloat16)
        y = jnp.dot(act, wd_s[...], preferred_element_type=jnp.float32) + bd_ref[0]
        y_ref[...] = y.reshape(y_ref.shape)

    @pl.when(j >= nused_ref[0])
    def _():
        y_ref[...] = jnp.zeros_like(y_ref)


def _experts(blk_expert, blk_first, n_used, xs, wgu, bgu, wd, bd):
    P = xs.shape[0]
    nb = P // MOE_BLOCK
    rows = pl.BlockSpec((MOE_BLOCK,) + ROW_TILE, lambda j, be, bf, nu: (j, 0, 0))
    return pl.pallas_call(
        _experts_body,
        grid_spec=pltpu.PrefetchScalarGridSpec(
            num_scalar_prefetch=3, grid=(nb,),
            in_specs=[
                rows,
                pl.BlockSpec((1, D_MODEL, 2 * D_EXPERT), lambda j, be, bf, nu: (be[j], 0, 0)),
                pl.BlockSpec((1, 1, 2 * D_EXPERT), lambda j, be, bf, nu: (be[j], 0, 0)),
                pl.BlockSpec((1, D_EXPERT, D_MODEL), lambda j, be, bf, nu: (be[j], 0, 0)),
                pl.BlockSpec((1, 1, D_MODEL), lambda j, be, bf, nu: (be[j], 0, 0)),
            ],
            out_specs=rows,
            scratch_shapes=[pltpu.VMEM((D_MODEL, 2 * D_EXPERT), jnp.bfloat16),
                            pltpu.VMEM((D_EXPERT, D_MODEL), jnp.bfloat16)]),
        out_shape=jax.ShapeDtypeStruct((P,) + ROW_TILE, jnp.float32),
        compiler_params=_cparams(("arbitrary",)),
        name="experts",
    )(blk_expert, blk_first, n_used, xs, wgu, bgu, wd, bd)


def _combine_body(dest_ref, y_ref, x1_ref, gate_ref, gn_ref, o_ref, ybuf, sem, *, tm):
    def row_copy(k, t, d):
        return pltpu.make_async_copy(y_ref.at[d], ybuf.at[k, t], sem)

    def issue(t, _):
        for k in range(MOE_TOP_K):
            row_copy(k, t, dest_ref[k, t]).start()
        return 0

    lax.fori_loop(0, tm, issue, 0)

    def drain(t, _):
        for k in range(MOE_TOP_K):
            row_copy(k, t, dest_ref[k, t]).wait()
        return 0

    lax.fori_loop(0, tm, drain, 0)

    g = gate_ref[...]
    acc = x1_ref[...]
    for k in range(MOE_TOP_K):
        acc = acc + g[:, k:k + 1] * ybuf[k].reshape(tm, D_MODEL)
    ms = jnp.mean(acc * acc, axis=-1, keepdims=True)
    o_ref[...] = acc * lax.rsqrt(ms + NORM_EPS) * gn_ref[...]


def _combine(dest, y, x1, gate_t, gn, tm):
    T = x1.shape[0]
    return pl.pallas_call(
        functools.partial(_combine_body, tm=tm),
        grid=(T // tm,),
        in_specs=[
            pl.BlockSpec((MOE_TOP_K, tm), lambda i: (0, i), memory_space=pltpu.SMEM),
            pl.BlockSpec(memory_space=pl.ANY),
            pl.BlockSpec((tm, D_MODEL), lambda i: (i, 0)),
            pl.BlockSpec((tm, MOE_TOP_K), lambda i: (i, 0)),
            pl.BlockSpec((1, D_MODEL), lambda i: (0, 0)),
        ],
        out_specs=pl.BlockSpec((tm, D_MODEL), lambda i: (i, 0)),
        out_shape=jax.ShapeDtypeStruct((T, D_MODEL), jnp.float32),
        scratch_shapes=[pltpu.VMEM((MOE_TOP_K, tm) + ROW_TILE, jnp.float32), pltpu.SemaphoreType.DMA(())],
        compiler_params=_cparams(("arbitrary",)),
        name="combine",
    )(dest, y, x1, gate_t, gn)


def _rope_tables(S):
    pos = jnp.arange(S, dtype=jnp.float32)[:, None]

    def cs(half):
        inv = ROPE_THETA ** (-jnp.arange(half, dtype=jnp.float32) / half)
        ang = pos * inv[None, :]
        return jnp.cos(ang), jnp.sin(ang)

    z = lambda n: jnp.zeros((S, n), jnp.float32)
    o = lambda n: jnp.ones((S, n), jnp.float32)
    cat = lambda *a: jnp.concatenate(a, axis=1)
    c, s = cs(MLA_ROPE // 2)
    mla = (cat(c, c, z(64)), cat(z(32), s, z(64)), cat(-s, z(96)))
    c, s = cs(DSA_HEAD_DIM // 2)
    dsa = (cat(c, c), cat(-s, s))
    c, s = cs(IDX_ROPE // 2)
    c1, sa1, sb1 = cat(c, c, o(32)), cat(z(16), s, z(32)), cat(-s, z(48))
    idx_q = (cat(c1, c1), cat(sa1, sa1), cat(sb1, sb1))
    idx_k = (cat(c1, o(64)), cat(sa1, z(64)), cat(sb1, z(64)))
    return mla, dsa + idx_q + idx_k


def _pack_w_in(w):
    sizes = (MLA_Q_LORA, MLA_KV_LORA, MLA_ROPE, 3 * DSA_HEADS * DSA_HEAD_DIM, IDX_HEADS * IDX_HEAD_DIM,
             IDX_HEAD_DIM, IDX_HEADS, 2 * D_MODEL)
    pts, acc = [], 0
    for s in sizes[:-1]:
        acc += s
        pts.append(acc)
    c_q, c_kv, k_r, qkv, q_idx, k_idx, w_idx, gates = jnp.split(w, pts, axis=1)
    z = lambda n: jnp.zeros((w.shape[0], n), w.dtype)
    packed = jnp.concatenate([gates, qkv, q_idx, k_idx, w_idx, z(LANES - IDX_HEAD_DIM - IDX_HEADS),
                              c_q, c_kv, k_r, z(LANES - MLA_ROPE)], axis=1)
    assert packed.shape[1] == D_IN_PACKED
    return packed.astype(jnp.bfloat16)


def _pack_w_uq(w):
    w = w.reshape(MLA_Q_LORA, MLA_HEADS, MLA_QK)
    w = jnp.pad(w, ((0, 0), (0, 0), (0, MLA_QK_PAD - MLA_QK)))
    return w.reshape(MLA_Q_LORA, MLA_HEADS * MLA_QK_PAD).astype(jnp.bfloat16)


def _tile(n, pref):
    t = min(n, pref)
    assert n % t == 0
    return t


def kernel(x, norm_attn_g, w_in, q_lora_g, kv_lora_g, w_uq, w_uk, w_uv, w_o_mla, w_o_dsa, b_gate, w_out,
           norm_ffn_g, w_router, b_router, w_gate_up, b_gate_up, w_down, b_down, norm_final_g):
    B, S, D = x.shape
    assert D == D_MODEL and S % CHUNK == 0 and norm_attn_g.shape[0] == 1
    T = B * S
    top_k = min(DSA_TOPK_MAX, S // 4)
    bf = jnp.bfloat16
    x2 = x.reshape(T, D)
    mla_tabs, dsa_tabs = _rope_tables(S)

    proj = _in_proj(x2, norm_attn_g, _pack_w_in(w_in[0]), _tile(T, 1024), D_IN_PACKED // 3)
    tk = _tile(S, 512)
    tq = _tile(S, 512)
    q, k, vt = _mla_prep(proj, q_lora_g, kv_lora_g, _pack_w_uq(w_uq[0]), w_uk[0].astype(bf), w_uv[0].T.astype(bf),
                         mla_tabs, S, tk)
    o_mla = _mla_attn(q, k, vt, B, S, tq, tk)
    qd, kd, vtd, qi, kia, kib, wit = _dsa_prep(proj, dsa_tabs, S, tk)
    o_dsa = _dsa_attn(qd, qi, wit, kd, vtd, kia, kib, B, S, tq, tk, top_k)
    x1, hn, logits_t = _merge(o_mla, o_dsa, proj, x2, w_o_mla[0].astype(bf), w_o_dsa[0].astype(bf),
                              w_out[0].astype(bf), b_gate, norm_ffn_g, w_router[0].T.astype(bf),
                              b_router[0][:, None], _tile(T, 512))

    tr = _tile(T, 512)
    eidx, gate, rank, cnt = _route(logits_t, tr)
    counts = cnt[:, 0].astype(jnp.int32)
    padded = ((counts + MOE_BLOCK - 1) // MOE_BLOCK) * MOE_BLOCK
    pad_end = jnp.cumsum(padded)
    pad_off = pad_end - padded
    P = T * MOE_TOP_K + N_EXPERTS * MOE_BLOCK
    nb = P // MOE_BLOCK
    blk_start = jnp.arange(nb, dtype=jnp.int32) * MOE_BLOCK
    blk_expert = jnp.minimum(jnp.sum(pad_end[None, :] <= blk_start[:, None], axis=1), N_EXPERTS - 1).astype(jnp.int32)
    blk_first = jnp.concatenate([jnp.ones((1,), jnp.int32), (blk_expert[1:] != blk_expert[:-1]).astype(jnp.int32)])
    n_used = (pad_end[-1:] // MOE_BLOCK).astype(jnp.int32)

    dest = _dest(pad_off.astype(jnp.int32), eidx, rank, tr)
    trailing = n_used[0] + jnp.arange(N_EXPERTS, dtype=jnp.int32)
    zero_start = jnp.concatenate([pad_end - MOE_BLOCK, jnp.minimum(trailing, nb - 1) * MOE_BLOCK]).astype(jnp.int32)
    zero_valid = jnp.concatenate([padded > 0, trailing < nb]).astype(jnp.int32)
    xs = _dispatch(zero_start, zero_valid, dest, hn, P, _tile(T, 128))
    y = _experts(blk_expert, blk_first, n_used, xs, w_gate_up[0], b_gate_up[0][:, None, :],
                 w_down[0], b_down[0][:, None, :])
    out = _combine(dest, y, x1, gate.T, norm_final_g[None, :], _tile(T, 128))
    return out.reshape(B, S, D)
```

```python
import functools

import jax
import jax.numpy as jnp
from jax import lax
from jax.experimental import pallas as pl
from jax.experimental.pallas import tpu as pltpu

D_MODEL = 1024
CHUNK = 64
ROPE_THETA = 10000.0
NORM_EPS = 1e-6

MLA_HEADS = 8
MLA_Q_LORA = 384
MLA_KV_LORA = 256
MLA_NOPE = 128
MLA_ROPE = 64
MLA_V = 128
MLA_QK = MLA_NOPE + MLA_ROPE
MLA_QK_PAD = 256

DSA_HEADS = 8
DSA_HEAD_DIM = 128
IDX_HEADS = 8
IDX_HEAD_DIM = 64
IDX_ROPE = 32
DSA_TOPK_MAX = 256

N_EXPERTS = 32
MOE_TOP_K = 4
D_EXPERT = 1024
SWIGLU_LIMIT = 7.0
SWIGLU_ALPHA = 1.702

LANES = 128
HEADS_PER_LOOP = 8
DSA_HEADS_PER_LOOP = 4
MOE_BLOCK = 512
DMA_PRIORITIES = 2
DRAIN_UNROLL = 8
ROW_TILE = (D_MODEL // LANES, LANES)
VMEM_LIMIT = 56 * 1024 * 1024

OFF_GATES = 0
OFF_Q = 2048
OFF_K = 3072
OFF_V = 4096
OFF_QIDX = 5120
OFF_KIDX = 5632
OFF_CQ = 5760
OFF_CKV = 6144
OFF_KR = 6400
D_IN_PACKED = 6528

NEG = -1e30
INT_MIN = -(2 ** 31)
KEY_NEG_INF = -2139095041

_NT = (((1,), (1,)), ((), ()))


def _cparams(sem):
    return pltpu.CompilerParams(dimension_semantics=sem, vmem_limit_bytes=VMEM_LIMIT)


def _rope3(x, c, sa, sb, shift):
    n = x.shape[-1]
    return x * c + pltpu.roll(x, shift, 1) * sa + pltpu.roll(x, n - shift, 1) * sb


def _inproj_body(x_ref, g_ref, w_ref, o_ref, h_scr):
    @pl.when(pl.program_id(1) == 0)
    def _():
        x = x_ref[...]
        ms = jnp.mean(x * x, axis=-1, keepdims=True)
        h_scr[...] = (x * lax.rsqrt(ms + NORM_EPS) * g_ref[...]).astype(jnp.bfloat16)

    o_ref[...] = jnp.dot(h_scr[...], w_ref[...], preferred_element_type=jnp.float32).astype(o_ref.dtype)


def _in_proj(x2, g, w_packed, tm, tn):
    T = x2.shape[0]
    return pl.pallas_call(
        _inproj_body,
        grid=(T // tm, D_IN_PACKED // tn),
        in_specs=[
            pl.BlockSpec((tm, D_MODEL), lambda i, j: (i, 0)),
            pl.BlockSpec((1, D_MODEL), lambda i, j: (0, 0)),
            pl.BlockSpec((D_MODEL, tn), lambda i, j: (0, j)),
        ],
        out_specs=pl.BlockSpec((tm, tn), lambda i, j: (i, j)),
        out_shape=jax.ShapeDtypeStruct((T, D_IN_PACKED), jnp.bfloat16),
        scratch_shapes=[pltpu.VMEM((tm, D_MODEL), jnp.bfloat16)],
        compiler_params=_cparams(("arbitrary", "arbitrary")),
        name="in_proj",
    )(x2, g, w_packed)


def _mla_prep_body(cq_ref, ckv_ref, kr_ref, gq_ref, gkv_ref, wq_ref, wk_ref, wv_ref,
                   c_ref, sa_ref, sb_ref, q_ref, k_ref, v_ref):
    def norm(ref, g_ref):
        x = ref[...].astype(jnp.float32)
        ms = jnp.mean(x * x, axis=-1, keepdims=True)
        return (x * lax.rsqrt(ms + NORM_EPS) * g_ref[...]).astype(jnp.bfloat16)

    cq = norm(cq_ref, gq_ref)
    ckv = norm(ckv_ref, gkv_ref)
    c, sa, sb = c_ref[...], sa_ref[...], sb_ref[...]
    scale = MLA_QK ** -0.5
    q_all = jnp.dot(cq, wq_ref[...], preferred_element_type=jnp.float32)
    kn = jnp.dot(ckv, wk_ref[...], preferred_element_type=jnp.float32)
    v_ref[0] = lax.dot_general(wv_ref[...], ckv, _NT, preferred_element_type=jnp.float32).astype(v_ref.dtype)
    kr = _rope3(kr_ref[...].astype(jnp.float32), c, sa, sb, MLA_ROPE // 2).astype(k_ref.dtype)
    for h in range(MLA_HEADS):
        o = h * MLA_QK_PAD
        q_ref[:, o:o + LANES] = (q_all[:, o:o + LANES] * scale).astype(q_ref.dtype)
        qr = _rope3(q_all[:, o + LANES:o + 2 * LANES], c, sa, sb, MLA_ROPE // 2)
        q_ref[:, o + LANES:o + 2 * LANES] = (qr * scale).astype(q_ref.dtype)
        k_ref[:, o:o + LANES] = kn[:, h * LANES:(h + 1) * LANES].astype(k_ref.dtype)
        k_ref[:, o + LANES:o + 2 * LANES] = kr


def _mla_prep(proj, gq, gkv, wq, wk, wv, tabs, S, tm):
    T = proj.shape[0]
    ns = S // tm
    full = lambda shape: pl.BlockSpec(shape, lambda i: (0, 0))
    tab = pl.BlockSpec((tm, LANES), lambda i: (i % ns, 0))
    hq = MLA_HEADS * MLA_QK_PAD
    return pl.pallas_call(
        _mla_prep_body,
        grid=(T // tm,),
        in_specs=[
            pl.BlockSpec((tm, MLA_Q_LORA), lambda i: (i, OFF_CQ // MLA_Q_LORA)),
            pl.BlockSpec((tm, MLA_KV_LORA), lambda i: (i, OFF_CKV // MLA_KV_LORA)),
            pl.BlockSpec((tm, LANES), lambda i: (i, OFF_KR // LANES)),
            full((1, MLA_Q_LORA)), full((1, MLA_KV_LORA)),
            full((MLA_Q_LORA, hq)), full((MLA_KV_LORA, MLA_HEADS * MLA_NOPE)),
            full((MLA_HEADS * MLA_V, MLA_KV_LORA)),
            tab, tab, tab,
        ],
        out_specs=[
            pl.BlockSpec((tm, hq), lambda i: (i, 0)),
            pl.BlockSpec((tm, hq), lambda i: (i, 0)),
            pl.BlockSpec((1, MLA_HEADS * MLA_V, tm), lambda i: (i, 0, 0)),
        ],
        out_shape=[
            jax.ShapeDtypeStruct((T, hq), jnp.bfloat16),
            jax.ShapeDtypeStruct((T, hq), jnp.bfloat16),
            jax.ShapeDtypeStruct((T // tm, MLA_HEADS * MLA_V, tm), jnp.bfloat16),
        ],
        compiler_params=_cparams(("arbitrary",)),
        name="mla_prep",
    )(proj, proj, proj, gq, gkv, wq, wk, wv, *tabs)


def _flash_t_init(tq, dv):
    return (jnp.full((1, tq), NEG, jnp.float32), jnp.zeros((1, tq), jnp.float32), jnp.zeros((dv, tq), jnp.float32))


def _flash_t_steps(carries, qs, kbs, vtbs, bias, s_scr):
    for n, (qh, kb) in enumerate(zip(qs, kbs)):
        s_scr[n] = lax.dot_general(kb, qh, _NT, preferred_element_type=jnp.float32)
    out = []
    for n, ((m, l, acc), vtb) in enumerate(zip(carries, vtbs)):
        s = s_scr[n]
        if bias is not None:
            s = s + bias
        m_new = jnp.maximum(m, jnp.max(s, axis=0, keepdims=True))
        alpha = jnp.exp(m - m_new)
        p = jnp.exp(s - m_new)
        l = alpha * l + jnp.sum(p, axis=0, keepdims=True)
        acc = alpha * acc + jnp.dot(vtb, p.astype(vtb.dtype), preferred_element_type=jnp.float32)
        out.append((m_new, l, acc))
    return tuple(out)


def _mla_attn_body(q_ref, k_ref, vt_ref, o_ref, s_scr, *, tq, tk):
    i = pl.program_id(1)
    q0 = i * tq
    nkv = (q0 + tq + tk - 1) // tk
    last = nkv - 1
    kc = (last * tk + lax.broadcasted_iota(jnp.int32, (tk, tq), 0)) // CHUNK
    qc = (q0 + lax.broadcasted_iota(jnp.int32, (tk, tq), 1)) // CHUNK
    last_bias = jnp.where(kc <= qc, 0.0, NEG)

    for h0 in range(0, MLA_HEADS, HEADS_PER_LOOP):
        heads = tuple(range(h0, h0 + HEADS_PER_LOOP))
        qs = [q_ref[:, h * MLA_QK_PAD:(h + 1) * MLA_QK_PAD] for h in heads]

        def step(j, carries, bias, heads=heads, qs=qs):
            off = pl.multiple_of(j * tk, tk)
            kbs = [k_ref[pl.ds(off, tk), h * MLA_QK_PAD:(h + 1) * MLA_QK_PAD] for h in heads]
            vtbs = [vt_ref[j, h * MLA_V:(h + 1) * MLA_V, :] for h in heads]
            return _flash_t_steps(carries, qs, kbs, vtbs, bias, s_scr)

        init = tuple(_flash_t_init(tq, MLA_V) for _ in heads)
        carries = lax.fori_loop(0, last, lambda j, c: step(j, c, None), init)
        carries = step(last, carries, last_bias)
        for h, (_, l, acc) in zip(heads, carries):
            o_ref[:, h * MLA_V:(h + 1) * MLA_V] = (acc / l).T.astype(o_ref.dtype)


def _mla_attn(q, k, vt, B, S, tq, tk):
    T = B * S
    nq = S // tq
    hq = MLA_HEADS * MLA_QK_PAD
    hv = MLA_HEADS * MLA_V
    assert vt.shape == (T // tk, hv, tk)
    return pl.pallas_call(
        functools.partial(_mla_attn_body, tq=tq, tk=tk),
        grid=(B, nq),
        in_specs=[
            pl.BlockSpec((tq, hq), lambda b, i: (b * nq + i, 0)),
            pl.BlockSpec((S, hq), lambda b, i: (b, 0)),
            pl.BlockSpec((S // tk, hv, tk), lambda b, i: (b, 0, 0)),
        ],
        out_specs=pl.BlockSpec((tq, hv), lambda b, i: (b * nq + i, 0)),
        out_shape=jax.ShapeDtypeStruct((T, hv), jnp.bfloat16),
        scratch_shapes=[pltpu.VMEM((HEADS_PER_LOOP, tk, tq), jnp.float32)],
        compiler_params=_cparams(("arbitrary", "arbitrary")),
        name="mla_attn",
    )(q, k, vt)


def _dsa_prep_body(q_ref, k_ref, v_ref, qi_ref, kx_ref, cd_ref, sd_ref, ci_ref, sai_ref, sbi_ref,
                   ck_ref, sak_ref, sbk_ref, qo_ref, ko_ref, vt_ref, qio_ref, kia_ref, kib_ref, wit_ref):
    cd, sd = cd_ref[...], sd_ref[...]
    for h in range(DSA_HEADS):
        sl = slice(h * LANES, (h + 1) * LANES)
        xq = q_ref[:, sl].astype(jnp.float32)
        xk = k_ref[:, sl].astype(jnp.float32)
        rq = xq * cd + pltpu.roll(xq, DSA_HEAD_DIM // 2, 1) * sd
        rk = xk * cd + pltpu.roll(xk, DSA_HEAD_DIM // 2, 1) * sd
        qo_ref[:, sl] = (rq * (DSA_HEAD_DIM ** -0.5)).astype(qo_ref.dtype)
        ko_ref[:, sl] = rk.astype(ko_ref.dtype)
        vt_ref[0, sl, :] = v_ref[:, sl].astype(jnp.float32).T.astype(vt_ref.dtype)
    ci, sai, sbi = ci_ref[...], sai_ref[...], sbi_ref[...]
    for p in range(IDX_HEADS * IDX_HEAD_DIM // LANES):
        sl = slice(p * LANES, (p + 1) * LANES)
        r = _rope3(qi_ref[:, sl].astype(jnp.float32), ci, sai, sbi, IDX_ROPE // 2)
        qio_ref[:, sl] = (r * (IDX_HEAD_DIM ** -0.5)).astype(qio_ref.dtype)
    kx = kx_ref[...].astype(jnp.float32)
    kr = _rope3(kx, ck_ref[...], sak_ref[...], sbk_ref[...], IDX_ROPE // 2)
    lane = lax.broadcasted_iota(jnp.int32, kr.shape, 1)
    ka = jnp.where(lane < IDX_HEAD_DIM, kr, 0.0)
    kia_ref[...] = ka.astype(kia_ref.dtype)
    kib_ref[...] = pltpu.roll(ka, IDX_HEAD_DIM, 1).astype(kib_ref.dtype)
    wit_ref[...] = kx.T[IDX_HEAD_DIM:IDX_HEAD_DIM + IDX_HEADS, :] * (IDX_HEADS ** -0.5)


def _dsa_prep(proj, tabs, S, tm):
    T = proj.shape[0]
    ns = S // tm
    tab = pl.BlockSpec((tm, LANES), lambda i: (i % ns, 0))
    hd = DSA_HEADS * DSA_HEAD_DIM
    hi = IDX_HEADS * IDX_HEAD_DIM
    row = lambda w, off: pl.BlockSpec((tm, w), lambda i: (i, off // w))
    out = lambda w: pl.BlockSpec((tm, w), lambda i: (i, 0))
    return pl.pallas_call(
        _dsa_prep_body,
        grid=(T // tm,),
        in_specs=[row(hd, OFF_Q), row(hd, OFF_K), row(hd, OFF_V), row(hi, OFF_QIDX), row(LANES, OFF_KIDX)] + [tab] * 8,
        out_specs=[out(hd), out(hd), pl.BlockSpec((1, hd, tm), lambda i: (i, 0, 0)), out(hi), out(LANES), out(LANES),
                   pl.BlockSpec((IDX_HEADS, tm), lambda i: (0, i))],
        out_shape=[
            jax.ShapeDtypeStruct((T, hd), jnp.bfloat16),
            jax.ShapeDtypeStruct((T, hd), jnp.bfloat16),
            jax.ShapeDtypeStruct((T // tm, hd, tm), jnp.bfloat16),
            jax.ShapeDtypeStruct((T, hi), jnp.bfloat16),
            jax.ShapeDtypeStruct((T, LANES), jnp.bfloat16),
            jax.ShapeDtypeStruct((T, LANES), jnp.bfloat16),
            jax.ShapeDtypeStruct((IDX_HEADS, T), jnp.float32),
        ],
        compiler_params=_cparams(("arbitrary",)),
        name="dsa_prep",
    )(proj, proj, proj, proj, proj, *tabs)


def _dsa_attn_body(qd_ref, qi_ref, wit_ref, kd_ref, vt_ref, kia_ref, kib_ref, o_ref, key_scr, bias_scr, s_scr,
                   *, tq, tk, top_k, idx_bits, seq_len):
    i = pl.program_id(1)
    q0 = i * tq
    nkv = (q0 + tq + tk - 1) // tk
    qchunk = (q0 + lax.broadcasted_iota(jnp.int32, (1, tq), 1)) // CHUNK
    row_k = lax.broadcasted_iota(jnp.int32, (tk, tq), 0)
    wit = wit_ref[...]

    def idx_block(j, _):
        off = pl.multiple_of(j * tk, tk)
        ka = kia_ref[pl.ds(off, tk), :]
        kb = kib_ref[pl.ds(off, tk), :]
        score = jnp.zeros((tk, tq), jnp.float32)
        for p in range(IDX_HEADS // 2):
            qp = qi_ref[:, p * LANES:(p + 1) * LANES]
            ra = lax.dot_general(ka, qp, _NT, preferred_element_type=jnp.float32)
            rb = lax.dot_general(kb, qp, _NT, preferred_element_type=jnp.float32)
            score = score + wit[2 * p:2 * p + 1, :] * jnp.maximum(ra, 0.0)
            score = score + wit[2 * p + 1:2 * p + 2, :] * jnp.maximum(rb, 0.0)
        adm = ((off + row_k) // CHUNK) <= qchunk
        score = jnp.where(adm, score + 0.0, -jnp.inf)
        b = pltpu.bitcast(score, jnp.int32)
        key_scr[j] = b ^ ((b >> 31) & 0x7FFFFFFF)
        return 0

    lax.fori_loop(0, nkv, idx_block, 0)

    def count(pred):
        def blk(j, part):
            c = pred(key_scr[j], j * tk + row_k).astype(jnp.int32)
            return part + jnp.sum(c, axis=0, keepdims=True)
        return lax.fori_loop(0, nkv, blk, jnp.zeros((1, tq), jnp.int32))

    tau = jnp.where(count(lambda k, _: k >= 0) >= top_k, 0, INT_MIN).astype(jnp.int32)

    def tau_bit(it, tau):
        cand = tau | (jnp.int32(1) << (30 - it))
        return jnp.where(count(lambda k, _: k >= cand) >= top_k, cand, tau)

    tau = lax.fori_loop(0, 31, tau_bit, tau)
    n_gt = count(lambda k, _: k > tau)
    n_ge = count(lambda k, _: k >= tau)
    need = top_k - n_gt

    def tie_search():
        def tie_bit(it, mi):
            step = jnp.int32(1) << (idx_bits - 1 - it)
            c = count(lambda k, idx: (k == tau) & (idx <= mi + step - 1))
            return jnp.where(c < need, mi + step, mi)
        return lax.fori_loop(0, idx_bits, tie_bit, jnp.zeros((1, tq), jnp.int32))

    partial_ties = (n_ge > top_k) & (tau > KEY_NEG_INF)
    mi = lax.cond(jnp.max(partial_ties.astype(jnp.int32)) > 0, tie_search,
                  lambda: jnp.full((1, tq), seq_len, jnp.int32))

    def bias_block(j, _):
        k = key_scr[j]
        idx = j * tk + row_k
        sel = (k > tau) | ((k == tau) & (idx <= mi))
        adm = (idx // CHUNK) <= qchunk
        bias_scr[j] = jnp.where(sel & adm, 0.0, NEG)
        return 0

    lax.fori_loop(0, nkv, bias_block, 0)

    for h0 in range(0, DSA_HEADS, DSA_HEADS_PER_LOOP):
        heads = tuple(range(h0, h0 + DSA_HEADS_PER_LOOP))
        qs = [qd_ref[:, h * DSA_HEAD_DIM:(h + 1) * DSA_HEAD_DIM] for h in heads]

        def att(j, carries, heads=heads, qs=qs):
            off = pl.multiple_of(j * tk, tk)
            sls = [slice(h * DSA_HEAD_DIM, (h + 1) * DSA_HEAD_DIM) for h in heads]
            kbs = [kd_ref[pl.ds(off, tk), sl] for sl in sls]
            vtbs = [vt_ref[j, sl, :] for sl in sls]
            return _flash_t_steps(carries, qs, kbs, vtbs, bias_scr[j], s_scr)

        carries = lax.fori_loop(0, nkv, att, tuple(_flash_t_init(tq, DSA_HEAD_DIM) for _ in heads))
        for h, (_, l, acc) in zip(heads, carries):
            o_ref[:, h * DSA_HEAD_DIM:(h + 1) * DSA_HEAD_DIM] = (acc / l).T.astype(o_ref.dtype)


def _dsa_attn(qd, qi, wit, kd, vt, kia, kib, B, S, tq, tk, top_k):
    T = B * S
    nq = S // tq
    hd = DSA_HEADS * DSA_HEAD_DIM
    hi = IDX_HEADS * IDX_HEAD_DIM
    idx_bits = max(1, (S - 1).bit_length())
    assert vt.shape == (T // tk, hd, tk)
    qrow = lambda w: pl.BlockSpec((tq, w), lambda b, i: (b * nq + i, 0))
    seq = lambda w: pl.BlockSpec((S, w), lambda b, i: (b, 0))
    return pl.pallas_call(
        functools.partial(_dsa_attn_body, tq=tq, tk=tk, top_k=top_k, idx_bits=idx_bits, seq_len=S),
        grid=(B, nq),
        in_specs=[qrow(hd), qrow(hi), pl.BlockSpec((IDX_HEADS, tq), lambda b, i: (0, b * nq + i)), seq(hd),
                  pl.BlockSpec((S // tk, hd, tk), lambda b, i: (b, 0, 0)), seq(LANES), seq(LANES)],
        out_specs=qrow(hd),
        out_shape=jax.ShapeDtypeStruct((T, hd), jnp.bfloat16),
        scratch_shapes=[pltpu.VMEM((S // tk, tk, tq), jnp.int32), pltpu.VMEM((S // tk, tk, tq), jnp.float32),
                        pltpu.VMEM((DSA_HEADS_PER_LOOP, tk, tq), jnp.float32)],
        compiler_params=_cparams(("arbitrary", "arbitrary")),
        name="dsa_attn",
    )(qd, qi, wit, kd, vt, kia, kib)


def _merge_body(oa_ref, ob_ref, gc_ref, x_ref, woa_ref, wob_ref, wout_ref, bg_ref, gf_ref, wr_ref, br_ref,
                x1_ref, hn_ref, lg_ref):
    ya = jnp.dot(oa_ref[...], woa_ref[...], preferred_element_type=jnp.float32)
    yb = jnp.dot(ob_ref[...], wob_ref[...], preferred_element_type=jnp.float32)
    g = jax.nn.sigmoid(gc_ref[...].astype(jnp.float32) + bg_ref[...])
    merged = g[:, :D_MODEL] * ya + g[:, D_MODEL:] * yb
    x1 = x_ref[...] + jnp.dot(merged.astype(jnp.bfloat16), wout_ref[...], preferred_element_type=jnp.float32)
    x1_ref[...] = x1
    ms = jnp.mean(x1 * x1, axis=-1, keepdims=True)
    hn = x1 * lax.rsqrt(ms + NORM_EPS) * gf_ref[...]
    hn_ref[...] = hn.reshape(hn_ref.shape)
    lg_ref[...] = lax.dot_general(wr_ref[...], hn.astype(jnp.bfloat16), _NT,
                                  preferred_element_type=jnp.float32) + br_ref[...]


def _merge(oa, ob, proj, x2, woa, wob, wout, bg, gf, wr_t, br, tm):
    T = x2.shape[0]
    row = lambda w: pl.BlockSpec((tm, w), lambda i: (i, 0))
    full = lambda shape: pl.BlockSpec(shape, lambda i: (0, 0))
    return pl.pallas_call(
        _merge_body,
        grid=(T // tm,),
        in_specs=[row(D_MODEL), row(D_MODEL), row(2 * D_MODEL), row(D_MODEL),
                  full((D_MODEL, D_MODEL)), full((D_MODEL, D_MODEL)), full((D_MODEL, D_MODEL)),
                  full((1, 2 * D_MODEL)), full((1, D_MODEL)), full((N_EXPERTS, D_MODEL)), full((N_EXPERTS, 1))],
        out_specs=[row(D_MODEL), pl.BlockSpec((tm,) + ROW_TILE, lambda i: (i, 0, 0)),
                   pl.BlockSpec((N_EXPERTS, tm), lambda i: (0, i))],
        out_shape=[
            jax.ShapeDtypeStruct((T, D_MODEL), jnp.float32),
            jax.ShapeDtypeStruct((T,) + ROW_TILE, jnp.float32),
            jax.ShapeDtypeStruct((N_EXPERTS, T), jnp.float32),
        ],
        compiler_params=_cparams(("arbitrary",)),
        name="merge",
    )(oa, ob, proj, x2, woa, wob, wout, bg, gf, wr_t, br)


def _route_body(lg_ref, eidx_ref, gate_ref, rank_ref, cnt_ref, carry_scr, *, tr):
    @pl.when(pl.program_id(0) == 0)
    def _():
        carry_scr[...] = jnp.zeros_like(carry_scr)

    vals = lg_ref[...]
    rows = lax.broadcasted_iota(jnp.int32, vals.shape, 0)
    tops, hots = [], []
    for k in range(MOE_TOP_K):
        m = jnp.max(vals, axis=0, keepdims=True)
        idx = jnp.min(jnp.where(vals == m, rows, N_EXPERTS), axis=0, keepdims=True)
        hot = rows == idx
        vals = jnp.where(hot, -jnp.inf, vals)
        tops.append(m)
        hots.append(hot)
        eidx_ref[k:k + 1, :] = idx
    es = [jnp.exp(m - tops[0]) for m in tops]
    den = es[0] + es[1] + es[2] + es[3]
    for k in range(MOE_TOP_K):
        gate_ref[k:k + 1, :] = es[k] / den
    sel = jnp.zeros(vals.shape, jnp.float32)
    for hot in hots:
        sel = sel + hot.astype(jnp.float32)
    before = (lax.broadcasted_iota(jnp.int32, (tr, tr), 0) < lax.broadcasted_iota(jnp.int32, (tr, tr), 1))
    prefix = jnp.dot(sel.astype(jnp.bfloat16), before.astype(jnp.bfloat16), preferred_element_type=jnp.float32)
    prefix = prefix + carry_scr[:, 0:1]
    for k in range(MOE_TOP_K):
        rank_ref[k:k + 1, :] = jnp.sum(jnp.where(hots[k], prefix, 0.0), axis=0, keepdims=True).astype(jnp.int32)
    carry_scr[...] = carry_scr[...] + jnp.sum(sel, axis=1, keepdims=True)
    cnt_ref[...] = carry_scr[...]


def _route(logits_t, tr):
    T = logits_t.shape[1]
    tok = pl.BlockSpec((MOE_TOP_K, tr), lambda i: (0, i))
    return pl.pallas_call(
        functools.partial(_route_body, tr=tr),
        grid=(T // tr,),
        in_specs=[pl.BlockSpec((N_EXPERTS, tr), lambda i: (0, i))],
        out_specs=[tok, tok, tok, pl.BlockSpec((N_EXPERTS, LANES), lambda i: (0, 0))],
        out_shape=[
            jax.ShapeDtypeStruct((MOE_TOP_K, T), jnp.int32),
            jax.ShapeDtypeStruct((MOE_TOP_K, T), jnp.float32),
            jax.ShapeDtypeStruct((MOE_TOP_K, T), jnp.int32),
            jax.ShapeDtypeStruct((N_EXPERTS, LANES), jnp.float32),
        ],
        scratch_shapes=[pltpu.VMEM((N_EXPERTS, LANES), jnp.float32)],
        compiler_params=_cparams(("arbitrary",)),
        name="route",
    )(logits_t)


def _dest_body(poff_ref, eidx_ref, rank_ref, dest_ref):
    e = eidx_ref[...]
    base = jnp.zeros(e.shape, jnp.int32)
    for x in range(N_EXPERTS):
        base = jnp.where(e == x, poff_ref[x], base)
    dest_ref[...] = base + rank_ref[...]


def _dest(pad_off, eidx, rank, tr):
    T = eidx.shape[1]
    tok = pl.BlockSpec((MOE_TOP_K, tr), lambda i, p: (0, i))
    return pl.pallas_call(
        _dest_body,
        grid_spec=pltpu.PrefetchScalarGridSpec(
            num_scalar_prefetch=1, grid=(T // tr,), in_specs=[tok, tok], out_specs=tok),
        out_shape=jax.ShapeDtypeStruct((MOE_TOP_K, T), jnp.int32),
        compiler_params=_cparams(("arbitrary",)),
        name="dest",
    )(pad_off, eidx, rank)


def _dispatch_body(zstart_ref, zvalid_ref, dest_ref, hn_ref, xs_ref, z_scr, zsem, sem, *, tm):
    @pl.when(pl.program_id(0) == 0)
    def _():
        z_scr[...] = jnp.zeros_like(z_scr)

        def zero_copy(n):
            return pltpu.make_async_copy(z_scr, xs_ref.at[pl.ds(zstart_ref[n], MOE_BLOCK)], zsem)

        def zissue(n, _):
            @pl.when(zvalid_ref[n] > 0)
            def _():
                zero_copy(n).start()
            return 0

        def zdrain(n, _):
            @pl.when(zvalid_ref[n] > 0)
            def _():
                zero_copy(n).wait()
            return 0

        lax.fori_loop(0, 2 * N_EXPERTS, zissue, 0)
        lax.fori_loop(0, 2 * N_EXPERTS, zdrain, 0)

    def row_copy(t, d):
        return pltpu.make_async_copy(hn_ref.at[t], xs_ref.at[d], sem)

    def issue(t, _):
        for k in range(MOE_TOP_K):
            row_copy(t, dest_ref[k, t]).start(priority=k % DMA_PRIORITIES)
        return 0

    lax.fori_loop(0, tm, issue, 0)

    def drain(t, _):
        for k in range(MOE_TOP_K):
            row_copy(t, dest_ref[k, t]).wait()
        return 0

    lax.fori_loop(0, tm, drain, 0, unroll=DRAIN_UNROLL)


def _dispatch(zero_start, zero_valid, dest, hn, P, tm):
    T = hn.shape[0]
    return pl.pallas_call(
        functools.partial(_dispatch_body, tm=tm),
        grid_spec=pltpu.PrefetchScalarGridSpec(
            num_scalar_prefetch=2, grid=(T // tm,),
            in_specs=[
                pl.BlockSpec((MOE_TOP_K, tm), lambda i, zs, zv: (0, i), memory_space=pltpu.SMEM),
                pl.BlockSpec((tm,) + ROW_TILE, lambda i, zs, zv: (i, 0, 0)),
            ],
            out_specs=pl.BlockSpec(memory_space=pl.ANY),
            scratch_shapes=[pltpu.VMEM((MOE_BLOCK,) + ROW_TILE, hn.dtype), pltpu.SemaphoreType.DMA(()),
                            pltpu.SemaphoreType.DMA(())]),
        out_shape=jax.ShapeDtypeStruct((P,) + ROW_TILE, hn.dtype),
        compiler_params=_cparams(("arbitrary",)),
        name="dispatch",
    )(zero_start, zero_valid, dest, hn)


def _experts_body(bexp_ref, first_ref, nused_ref, xs_ref, wgu_ref, bgu_ref, wd_ref, bd_ref, y_ref, wgu_s, wd_s):
    j = pl.program_id(0)

    @pl.when(first_ref[j] > 0)
    def _():
        wgu_s[...] = wgu_ref[0].astype(jnp.bfloat16)
        wd_s[...] = wd_ref[0].astype(jnp.bfloat16)

    @pl.when(j < nused_ref[0])
    def _():
        xb = xs_ref[...].reshape(MOE_BLOCK, D_MODEL).astype(jnp.bfloat16)
        h = jnp.dot(xb, wgu_s[...], preferred_element_type=jnp.float32) + bgu_ref[0]
        gate = jnp.minimum(h[:, :D_EXPERT], SWIGLU_LIMIT)
        lin = jnp.clip(h[:, D_EXPERT:], -SWIGLU_LIMIT, SWIGLU_LIMIT)
        glu = gate * jax.nn.sigmoid(SWIGLU_ALPHA * gate)
        act = ((lin + 1.0) * glu).astype(jnp.bfloat16)
        y = jnp.dot(act, wd_s[...], preferred_element_type=jnp.float32) + bd_ref[0]
        y_ref[...] = y.reshape(y_ref.shape)

    @pl.when(j >= nused_ref[0])
    def _():
        y_ref[...] = jnp.zeros_like(y_ref)


def _experts(blk_expert, blk_first, n_used, xs, wgu, bgu, wd, bd):
    P = xs.shape[0]
    nb = P // MOE_BLOCK
    rows = pl.BlockSpec((MOE_BLOCK,) + ROW_TILE, lambda j, be, bf, nu: (j, 0, 0))
    return pl.pallas_call(
        _experts_body,
        grid_spec=pltpu.PrefetchScalarGridSpec(
            num_scalar_prefetch=3, grid=(nb,),
            in_specs=[
                rows,
                pl.BlockSpec((1, D_MODEL, 2 * D_EXPERT), lambda j, be, bf, nu: (be[j], 0, 0)),
                pl.BlockSpec((1, 1, 2 * D_EXPERT), lambda j, be, bf, nu: (be[j], 0, 0)),
                pl.BlockSpec((1, D_EXPERT, D_MODEL), lambda j, be, bf, nu: (be[j], 0, 0)),
                pl.BlockSpec((1, 1, D_MODEL), lambda j, be, bf, nu: (be[j], 0, 0)),
            ],
            out_specs=rows,
            scratch_shapes=[pltpu.VMEM((D_MODEL, 2 * D_EXPERT), jnp.bfloat16),
                            pltpu.VMEM((D_EXPERT, D_MODEL), jnp.bfloat16)]),
        out_shape=jax.ShapeDtypeStruct((P,) + ROW_TILE, jnp.float32),
        compiler_params=_cparams(("arbitrary",)),
        name="experts",
    )(blk_expert, blk_first, n_used, xs, wgu, bgu, wd, bd)


def _combine_body(dest_ref, dest_next_ref, y_ref, x1_ref, gate_ref, gn_ref, o_ref, ybuf, sems, *, tm):
    i = pl.program_id(0)
    slot = i % 2

    def row_copy(dref, s, k, t):
        return pltpu.make_async_copy(y_ref.at[dref[k, t]], ybuf.at[s, k, t], sems.at[s])

    def issue_rows(dref, s):
        def issue(t, _):
            for k in range(MOE_TOP_K):
                row_copy(dref, s, k, t).start(priority=k % DMA_PRIORITIES)
            return 0
        lax.fori_loop(0, tm, issue, 0)

    @pl.when(i == 0)
    def _():
        issue_rows(dest_ref, 0)

    @pl.when(i + 1 < pl.num_programs(0))
    def _():
        issue_rows(dest_next_ref, 1 - slot)

    def drain(t, _):
        for k in range(MOE_TOP_K):
            row_copy(dest_ref, slot, k, t).wait()
        return 0

    lax.fori_loop(0, tm, drain, 0, unroll=DRAIN_UNROLL)

    g = gate_ref[...]
    acc = x1_ref[...]
    for k in range(MOE_TOP_K):
        acc = acc + g[:, k:k + 1] * ybuf[slot, k].reshape(tm, D_MODEL)
    ms = jnp.mean(acc * acc, axis=-1, keepdims=True)
    o_ref[...] = acc * lax.rsqrt(ms + NORM_EPS) * gn_ref[...]


def _combine(dest, y, x1, gate_t, gn, tm):
    T = x1.shape[0]
    n = T // tm
    return pl.pallas_call(
        functools.partial(_combine_body, tm=tm),
        grid=(n,),
        in_specs=[
            pl.BlockSpec((MOE_TOP_K, tm), lambda i: (0, i), memory_space=pltpu.SMEM),
            pl.BlockSpec((MOE_TOP_K, tm), lambda i: (0, jnp.minimum(i + 1, n - 1)), memory_space=pltpu.SMEM),
            pl.BlockSpec(memory_space=pl.ANY),
            pl.BlockSpec((tm, D_MODEL), lambda i: (i, 0)),
            pl.BlockSpec((tm, MOE_TOP_K), lambda i: (i, 0)),
            pl.BlockSpec((1, D_MODEL), lambda i: (0, 0)),
        ],
        out_specs=pl.BlockSpec((tm, D_MODEL), lambda i: (i, 0)),
        out_shape=jax.ShapeDtypeStruct((T, D_MODEL), jnp.float32),
        scratch_shapes=[pltpu.VMEM((2, MOE_TOP_K, tm) + ROW_TILE, jnp.float32), pltpu.SemaphoreType.DMA((2,))],
        compiler_params=_cparams(("arbitrary",)),
        name="combine",
    )(dest, dest, y, x1, gate_t, gn)


def _rope_tables(S):
    pos = jnp.arange(S, dtype=jnp.float32)[:, None]

    def cs(half):
        inv = ROPE_THETA ** (-jnp.arange(half, dtype=jnp.float32) / half)
        ang = pos * inv[None, :]
        return jnp.cos(ang), jnp.sin(ang)

    z = lambda n: jnp.zeros((S, n), jnp.float32)
    o = lambda n: jnp.ones((S, n), jnp.float32)
    cat = lambda *a: jnp.concatenate(a, axis=1)
    c, s = cs(MLA_ROPE // 2)
    mla = (cat(c, c, z(64)), cat(z(32), s, z(64)), cat(-s, z(96)))
    c, s = cs(DSA_HEAD_DIM // 2)
    dsa = (cat(c, c), cat(-s, s))
    c, s = cs(IDX_ROPE // 2)
    c1, sa1, sb1 = cat(c, c, o(32)), cat(z(16), s, z(32)), cat(-s, z(48))
    idx_q = (cat(c1, c1), cat(sa1, sa1), cat(sb1, sb1))
    idx_k = (cat(c1, o(64)), cat(sa1, z(64)), cat(sb1, z(64)))
    return mla, dsa + idx_q + idx_k


def _pack_w_in(w):
    sizes = (MLA_Q_LORA, MLA_KV_LORA, MLA_ROPE, 3 * DSA_HEADS * DSA_HEAD_DIM, IDX_HEADS * IDX_HEAD_DIM,
             IDX_HEAD_DIM, IDX_HEADS, 2 * D_MODEL)
    pts, acc = [], 0
    for s in sizes[:-1]:
        acc += s
        pts.append(acc)
    c_q, c_kv, k_r, qkv, q_idx, k_idx, w_idx, gates = jnp.split(w, pts, axis=1)
    z = lambda n: jnp.zeros((w.shape[0], n), w.dtype)
    packed = jnp.concatenate([gates, qkv, q_idx, k_idx, w_idx, z(LANES - IDX_HEAD_DIM - IDX_HEADS),
                              c_q, c_kv, k_r, z(LANES - MLA_ROPE)], axis=1)
    assert packed.shape[1] == D_IN_PACKED
    return packed.astype(jnp.bfloat16)


def _pack_w_uq(w):
    w = w.reshape(MLA_Q_LORA, MLA_HEADS, MLA_QK)
    w = jnp.pad(w, ((0, 0), (0, 0), (0, MLA_QK_PAD - MLA_QK)))
    return w.reshape(MLA_Q_LORA, MLA_HEADS * MLA_QK_PAD).astype(jnp.bfloat16)


def _tile(n, pref):
    t = min(n, pref)
    assert n % t == 0
    return t


def kernel(x, norm_attn_g, w_in, q_lora_g, kv_lora_g, w_uq, w_uk, w_uv, w_o_mla, w_o_dsa, b_gate, w_out,
           norm_ffn_g, w_router, b_router, w_gate_up, b_gate_up, w_down, b_down, norm_final_g):
    B, S, D = x.shape
    assert D == D_MODEL and S % CHUNK == 0 and norm_attn_g.shape[0] == 1
    T = B * S
    top_k = min(DSA_TOPK_MAX, S // 4)
    bf = jnp.bfloat16
    x2 = x.reshape(T, D)
    mla_tabs, dsa_tabs = _rope_tables(S)

    proj = _in_proj(x2, norm_attn_g, _pack_w_in(w_in[0]), _tile(T, 1024), D_IN_PACKED // 3)
    tk = _tile(S, 512)
    tq = _tile(S, 512)
    q, k, vt = _mla_prep(proj, q_lora_g, kv_lora_g, _pack_w_uq(w_uq[0]), w_uk[0].astype(bf), w_uv[0].T.astype(bf),
                         mla_tabs, S, tk)
    o_mla = _mla_attn(q, k, vt, B, S, tq, tk)
    qd, kd, vtd, qi, kia, kib, wit = _dsa_prep(proj, dsa_tabs, S, tk)
    o_dsa = _dsa_attn(qd, qi, wit, kd, vtd, kia, kib, B, S, tq, tk, top_k)
    x1, hn, logits_t = _merge(o_mla, o_dsa, proj, x2, w_o_mla[0].astype(bf), w_o_dsa[0].astype(bf),
                              w_out[0].astype(bf), b_gate, norm_ffn_g, w_router[0].T.astype(bf),
                              b_router[0][:, None], _tile(T, 512))

    tr = _tile(T, 512)
    eidx, gate, rank, cnt = _route(logits_t, tr)
    counts = cnt[:, 0].astype(jnp.int32)
    padded = ((counts + MOE_BLOCK - 1) // MOE_BLOCK) * MOE_BLOCK
    pad_end = jnp.cumsum(padded)
    pad_off = pad_end - padded
    P = T * MOE_TOP_K + N_EXPERTS * MOE_BLOCK
    nb = P // MOE_BLOCK
    blk_start = jnp.arange(nb, dtype=jnp.int32) * MOE_BLOCK
    blk_expert = jnp.minimum(jnp.sum(pad_end[None, :] <= blk_start[:, None], axis=1), N_EXPERTS - 1).astype(jnp.int32)
    blk_first = jnp.concatenate([jnp.ones((1,), jnp.int32), (blk_expert[1:] != blk_expert[:-1]).astype(jnp.int32)])
    n_used = (pad_end[-1:] // MOE_BLOCK).astype(jnp.int32)

    dest = _dest(pad_off.astype(jnp.int32), eidx, rank, tr)
    trailing = n_used[0] + jnp.arange(N_EXPERTS, dtype=jnp.int32)
    zero_start = jnp.concatenate([pad_end - MOE_BLOCK, jnp.minimum(trailing, nb - 1) * MOE_BLOCK]).astype(jnp.int32)
    zero_valid = jnp.concatenate([padded > 0, trailing < nb]).astype(jnp.int32)
    xs = _dispatch(zero_start, zero_valid, dest, hn, P, _tile(T, 128))
    y = _experts(blk_expert, blk_first, n_used, xs, w_gate_up[0], b_gate_up[0][:, None, :],
                 w_down[0], b_down[0][:, None, :])
    out = _combine(dest, y, x1, gate.T, norm_final_g[None, :], _tile(T, 128))
    return out.reshape(B, S, D)
```

```python
import functools

import jax
import jax.numpy as jnp
from jax import lax
from jax.experimental import pallas as pl
from jax.experimental.pallas import tpu as pltpu

D_MODEL = 1024
CHUNK = 64
ROPE_THETA = 10000.0
NORM_EPS = 1e-6

MLA_HEADS = 8
MLA_Q_LORA = 384
MLA_KV_LORA = 256
MLA_NOPE = 128
MLA_ROPE = 64
MLA_V = 128
MLA_QK = MLA_NOPE + MLA_ROPE
MLA_QK_PAD = 256

DSA_HEADS = 8
DSA_HEAD_DIM = 128
IDX_HEADS = 8
IDX_HEAD_DIM = 64
IDX_ROPE = 32
DSA_TOPK_MAX = 256

N_EXPERTS = 32
MOE_TOP_K = 4
D_EXPERT = 1024
SWIGLU_LIMIT = 7.0
SWIGLU_ALPHA = 1.702

LANES = 128
HEADS_PER_LOOP = 8
DSA_HEADS_PER_LOOP = 4
MOE_BLOCK = 512
DISPATCH_SLOTS = 3
DMA_PRIORITIES = 2
DRAIN_UNROLL = 8
ROW_TILE = (D_MODEL // LANES, LANES)
VMEM_LIMIT = 56 * 1024 * 1024

OFF_GATES = 0
OFF_Q = 2048
OFF_K = 3072
OFF_V = 4096
OFF_QIDX = 5120
OFF_KIDX = 5632
OFF_CQ = 5760
OFF_CKV = 6144
OFF_KR = 6400
D_IN_PACKED = 6528

LOG2E = 1.4426950408889634
NEG = -1e30
INT_MIN = -(2 ** 31)
KEY_NEG_INF = -2139095041

_NT = (((1,), (1,)), ((), ()))


def _cparams(sem):
    return pltpu.CompilerParams(dimension_semantics=sem, vmem_limit_bytes=VMEM_LIMIT)


def _rope3(x, c, sa, sb, shift):
    n = x.shape[-1]
    return x * c + pltpu.roll(x, shift, 1) * sa + pltpu.roll(x, n - shift, 1) * sb


def _inproj_body(x_ref, g_ref, w_ref, o_ref, h_scr):
    @pl.when(pl.program_id(1) == 0)
    def _():
        x = x_ref[...]
        ms = jnp.mean(x * x, axis=-1, keepdims=True)
        h_scr[...] = (x * lax.rsqrt(ms + NORM_EPS) * g_ref[...]).astype(jnp.bfloat16)

    o_ref[...] = jnp.dot(h_scr[...], w_ref[...], preferred_element_type=jnp.float32).astype(o_ref.dtype)


def _in_proj(x2, g, w_packed, tm, tn):
    T = x2.shape[0]
    return pl.pallas_call(
        _inproj_body,
        grid=(T // tm, D_IN_PACKED // tn),
        in_specs=[
            pl.BlockSpec((tm, D_MODEL), lambda i, j: (i, 0)),
            pl.BlockSpec((1, D_MODEL), lambda i, j: (0, 0)),
            pl.BlockSpec((D_MODEL, tn), lambda i, j: (0, j)),
        ],
        out_specs=pl.BlockSpec((tm, tn), lambda i, j: (i, j)),
        out_shape=jax.ShapeDtypeStruct((T, D_IN_PACKED), jnp.bfloat16),
        scratch_shapes=[pltpu.VMEM((tm, D_MODEL), jnp.bfloat16)],
        compiler_params=_cparams(("arbitrary", "arbitrary")),
        name="in_proj",
    )(x2, g, w_packed)


def _mla_prep_body(cq_ref, ckv_ref, kr_ref, gq_ref, gkv_ref, wq_ref, wk_ref, wv_ref,
                   c_ref, sa_ref, sb_ref, q_ref, k_ref, v_ref):
    def norm(ref, g_ref):
        x = ref[...].astype(jnp.float32)
        ms = jnp.mean(x * x, axis=-1, keepdims=True)
        return (x * lax.rsqrt(ms + NORM_EPS) * g_ref[...]).astype(jnp.bfloat16)

    cq = norm(cq_ref, gq_ref)
    ckv = norm(ckv_ref, gkv_ref)
    c, sa, sb = c_ref[...], sa_ref[...], sb_ref[...]
    scale = MLA_QK ** -0.5 * LOG2E
    q_all = jnp.dot(cq, wq_ref[...], preferred_element_type=jnp.float32)
    kn = jnp.dot(ckv, wk_ref[...], preferred_element_type=jnp.float32)
    v_ref[0] = lax.dot_general(wv_ref[...], ckv, _NT, preferred_element_type=jnp.float32).astype(v_ref.dtype)
    kr = _rope3(kr_ref[...].astype(jnp.float32), c, sa, sb, MLA_ROPE // 2).astype(k_ref.dtype)
    for h in range(MLA_HEADS):
        o = h * MLA_QK_PAD
        q_ref[:, o:o + LANES] = (q_all[:, o:o + LANES] * scale).astype(q_ref.dtype)
        qr = _rope3(q_all[:, o + LANES:o + 2 * LANES], c, sa, sb, MLA_ROPE // 2)
        q_ref[:, o + LANES:o + 2 * LANES] = (qr * scale).astype(q_ref.dtype)
        k_ref[:, o:o + LANES] = kn[:, h * LANES:(h + 1) * LANES].astype(k_ref.dtype)
        k_ref[:, o + LANES:o + 2 * LANES] = kr


def _mla_prep(proj, gq, gkv, wq, wk, wv, tabs, S, tm):
    T = proj.shape[0]
    ns = S // tm
    full = lambda shape: pl.BlockSpec(shape, lambda i: (0, 0))
    tab = pl.BlockSpec((tm, LANES), lambda i: (i % ns, 0))
    hq = MLA_HEADS * MLA_QK_PAD
    return pl.pallas_call(
        _mla_prep_body,
        grid=(T // tm,),
        in_specs=[
            pl.BlockSpec((tm, MLA_Q_LORA), lambda i: (i, OFF_CQ // MLA_Q_LORA)),
            pl.BlockSpec((tm, MLA_KV_LORA), lambda i: (i, OFF_CKV // MLA_KV_LORA)),
            pl.BlockSpec((tm, LANES), lambda i: (i, OFF_KR // LANES)),
            full((1, MLA_Q_LORA)), full((1, MLA_KV_LORA)),
            full((MLA_Q_LORA, hq)), full((MLA_KV_LORA, MLA_HEADS * MLA_NOPE)),
            full((MLA_HEADS * MLA_V, MLA_KV_LORA)),
            tab, tab, tab,
        ],
        out_specs=[
            pl.BlockSpec((tm, hq), lambda i: (i, 0)),
            pl.BlockSpec((tm, hq), lambda i: (i, 0)),
            pl.BlockSpec((1, MLA_HEADS * MLA_V, tm), lambda i: (i, 0, 0)),
        ],
        out_shape=[
            jax.ShapeDtypeStruct((T, hq), jnp.bfloat16),
            jax.ShapeDtypeStruct((T, hq), jnp.bfloat16),
            jax.ShapeDtypeStruct((T // tm, MLA_HEADS * MLA_V, tm), jnp.bfloat16),
        ],
        compiler_params=_cparams(("arbitrary",)),
        name="mla_prep",
    )(proj, proj, proj, gq, gkv, wq, wk, wv, *tabs)


def _flash_t_init(tq, dv):
    return (jnp.full((1, tq), NEG, jnp.float32), jnp.zeros((1, tq), jnp.float32), jnp.zeros((dv, tq), jnp.float32))


def _flash_t_steps(carries, qs, kbs, vtbs, bias, s_scr):
    for n, (qh, kb) in enumerate(zip(qs, kbs)):
        s_scr[n] = lax.dot_general(kb, qh, _NT, preferred_element_type=jnp.float32)
    out = []
    for n, ((m, l, acc), vtb) in enumerate(zip(carries, vtbs)):
        s = s_scr[n]
        if bias is not None:
            s = s + bias
        m_new = jnp.maximum(m, jnp.max(s, axis=0, keepdims=True))
        alpha = jnp.exp2(m - m_new)
        p = jnp.exp2(s - m_new)
        l = alpha * l + jnp.sum(p, axis=0, keepdims=True)
        acc = alpha * acc + jnp.dot(vtb, p.astype(vtb.dtype), preferred_element_type=jnp.float32)
        out.append((m_new, l, acc))
    return tuple(out)


def _mla_attn_body(q_ref, k_ref, vt_ref, o_ref, s_scr, *, tq, tk):
    i = pl.program_id(1)
    q0 = i * tq
    nkv = (q0 + tq + tk - 1) // tk
    last = nkv - 1
    kc = (last * tk + lax.broadcasted_iota(jnp.int32, (tk, tq), 0)) // CHUNK
    qc = (q0 + lax.broadcasted_iota(jnp.int32, (tk, tq), 1)) // CHUNK
    last_bias = jnp.where(kc <= qc, 0.0, NEG)

    for h0 in range(0, MLA_HEADS, HEADS_PER_LOOP):
        heads = tuple(range(h0, h0 + HEADS_PER_LOOP))
        qs = [q_ref[:, h * MLA_QK_PAD:(h + 1) * MLA_QK_PAD] for h in heads]

        def step(j, carries, bias, heads=heads, qs=qs):
            off = pl.multiple_of(j * tk, tk)
            kbs = [k_ref[pl.ds(off, tk), h * MLA_QK_PAD:(h + 1) * MLA_QK_PAD] for h in heads]
            vtbs = [vt_ref[j, h * MLA_V:(h + 1) * MLA_V, :] for h in heads]
            return _flash_t_steps(carries, qs, kbs, vtbs, bias, s_scr)

        init = tuple(_flash_t_init(tq, MLA_V) for _ in heads)
        carries = lax.fori_loop(0, last, lambda j, c: step(j, c, None), init)
        carries = step(last, carries, last_bias)
        for h, (_, l, acc) in zip(heads, carries):
            o_ref[:, h * MLA_V:(h + 1) * MLA_V] = (acc / l).T.astype(o_ref.dtype)


def _mla_attn(q, k, vt, B, S, tq, tk):
    T = B * S
    nq = S // tq
    hq = MLA_HEADS * MLA_QK_PAD
    hv = MLA_HEADS * MLA_V
    assert vt.shape == (T // tk, hv, tk)
    return pl.pallas_call(
        functools.partial(_mla_attn_body, tq=tq, tk=tk),
        grid=(B, nq),
        in_specs=[
            pl.BlockSpec((tq, hq), lambda b, i: (b * nq + i, 0)),
            pl.BlockSpec((S, hq), lambda b, i: (b, 0)),
            pl.BlockSpec((S // tk, hv, tk), lambda b, i: (b, 0, 0)),
        ],
        out_specs=pl.BlockSpec((tq, hv), lambda b, i: (b * nq + i, 0)),
        out_shape=jax.ShapeDtypeStruct((T, hv), jnp.bfloat16),
        scratch_shapes=[pltpu.VMEM((HEADS_PER_LOOP, tk, tq), jnp.float32)],
        compiler_params=_cparams(("arbitrary", "arbitrary")),
        name="mla_attn",
    )(q, k, vt)


def _dsa_prep_body(q_ref, k_ref, v_ref, qi_ref, kx_ref, cd_ref, sd_ref, ci_ref, sai_ref, sbi_ref,
                   ck_ref, sak_ref, sbk_ref, qo_ref, ko_ref, vt_ref, qio_ref, kia_ref, kib_ref, wit_ref):
    cd, sd = cd_ref[...], sd_ref[...]
    for h in range(DSA_HEADS):
        sl = slice(h * LANES, (h + 1) * LANES)
        xq = q_ref[:, sl].astype(jnp.float32)
        xk = k_ref[:, sl].astype(jnp.float32)
        rq = xq * cd + pltpu.roll(xq, DSA_HEAD_DIM // 2, 1) * sd
        rk = xk * cd + pltpu.roll(xk, DSA_HEAD_DIM // 2, 1) * sd
        qo_ref[:, sl] = (rq * (DSA_HEAD_DIM ** -0.5 * LOG2E)).astype(qo_ref.dtype)
        ko_ref[:, sl] = rk.astype(ko_ref.dtype)
        vt_ref[0, sl, :] = v_ref[:, sl].astype(jnp.float32).T.astype(vt_ref.dtype)
    ci, sai, sbi = ci_ref[...], sai_ref[...], sbi_ref[...]
    for p in range(IDX_HEADS * IDX_HEAD_DIM // LANES):
        sl = slice(p * LANES, (p + 1) * LANES)
        r = _rope3(qi_ref[:, sl].astype(jnp.float32), ci, sai, sbi, IDX_ROPE // 2)
        qio_ref[:, sl] = (r * (IDX_HEAD_DIM ** -0.5)).astype(qio_ref.dtype)
    kx = kx_ref[...].astype(jnp.float32)
    kr = _rope3(kx, ck_ref[...], sak_ref[...], sbk_ref[...], IDX_ROPE // 2)
    lane = lax.broadcasted_iota(jnp.int32, kr.shape, 1)
    ka = jnp.where(lane < IDX_HEAD_DIM, kr, 0.0)
    kia_ref[...] = ka.astype(kia_ref.dtype)
    kib_ref[...] = pltpu.roll(ka, IDX_HEAD_DIM, 1).astype(kib_ref.dtype)
    wit_ref[...] = kx.T[IDX_HEAD_DIM:IDX_HEAD_DIM + IDX_HEADS, :] * (IDX_HEADS ** -0.5)


def _dsa_prep(proj, tabs, S, tm):
    T = proj.shape[0]
    ns = S // tm
    tab = pl.BlockSpec((tm, LANES), lambda i: (i % ns, 0))
    hd = DSA_HEADS * DSA_HEAD_DIM
    hi = IDX_HEADS * IDX_HEAD_DIM
    row = lambda w, off: pl.BlockSpec((tm, w), lambda i: (i, off // w))
    out = lambda w: pl.BlockSpec((tm, w), lambda i: (i, 0))
    return pl.pallas_call(
        _dsa_prep_body,
        grid=(T // tm,),
        in_specs=[row(hd, OFF_Q), row(hd, OFF_K), row(hd, OFF_V), row(hi, OFF_QIDX), row(LANES, OFF_KIDX)] + [tab] * 8,
        out_specs=[out(hd), out(hd), pl.BlockSpec((1, hd, tm), lambda i: (i, 0, 0)), out(hi), out(LANES), out(LANES),
                   pl.BlockSpec((IDX_HEADS, tm), lambda i: (0, i))],
        out_shape=[
            jax.ShapeDtypeStruct((T, hd), jnp.bfloat16),
            jax.ShapeDtypeStruct((T, hd), jnp.bfloat16),
            jax.ShapeDtypeStruct((T // tm, hd, tm), jnp.bfloat16),
            jax.ShapeDtypeStruct((T, hi), jnp.bfloat16),
            jax.ShapeDtypeStruct((T, LANES), jnp.bfloat16),
            jax.ShapeDtypeStruct((T, LANES), jnp.bfloat16),
            jax.ShapeDtypeStruct((IDX_HEADS, T), jnp.float32),
        ],
        compiler_params=_cparams(("arbitrary",)),
        name="dsa_prep",
    )(proj, proj, proj, proj, proj, *tabs)


def _dsa_attn_body(qd_ref, qi_ref, wit_ref, kd_ref, vt_ref, kia_ref, kib_ref, o_ref, key_scr, bias_scr, s_scr,
                   *, tq, tk, top_k, idx_bits, seq_len):
    i = pl.program_id(1)
    q0 = i * tq
    nkv = (q0 + tq + tk - 1) // tk
    qchunk = (q0 + lax.broadcasted_iota(jnp.int32, (1, tq), 1)) // CHUNK
    row_k = lax.broadcasted_iota(jnp.int32, (tk, tq), 0)
    wit = wit_ref[...]

    def idx_block(j, _):
        off = pl.multiple_of(j * tk, tk)
        ka = kia_ref[pl.ds(off, tk), :]
        kb = kib_ref[pl.ds(off, tk), :]
        score = jnp.zeros((tk, tq), jnp.float32)
        for p in range(IDX_HEADS // 2):
            qp = qi_ref[:, p * LANES:(p + 1) * LANES]
            ra = lax.dot_general(ka, qp, _NT, preferred_element_type=jnp.float32)
            rb = lax.dot_general(kb, qp, _NT, preferred_element_type=jnp.float32)
            score = score + wit[2 * p:2 * p + 1, :] * jnp.maximum(ra, 0.0)
            score = score + wit[2 * p + 1:2 * p + 2, :] * jnp.maximum(rb, 0.0)
        adm = ((off + row_k) // CHUNK) <= qchunk
        score = jnp.where(adm, score + 0.0, -jnp.inf)
        b = pltpu.bitcast(score, jnp.int32)
        key_scr[j] = b ^ ((b >> 31) & 0x7FFFFFFF)
        return 0

    lax.fori_loop(0, nkv, idx_block, 0)

    def count(pred):
        def blk(j, part):
            c = pred(key_scr[j], j * tk + row_k).astype(jnp.int32)
            return part + jnp.sum(c, axis=0, keepdims=True)
        return lax.fori_loop(0, nkv, blk, jnp.zeros((1, tq), jnp.int32))

    tau = jnp.where(count(lambda k, _: k >= 0) >= top_k, 0, INT_MIN).astype(jnp.int32)

    def tau_bit(it, tau):
        cand = tau | (jnp.int32(1) << (30 - it))
        return jnp.where(count(lambda k, _: k >= cand) >= top_k, cand, tau)

    tau = lax.fori_loop(0, 31, tau_bit, tau)
    n_gt = count(lambda k, _: k > tau)
    n_ge = count(lambda k, _: k >= tau)
    need = top_k - n_gt

    def tie_search():
        def tie_bit(it, mi):
            step = jnp.int32(1) << (idx_bits - 1 - it)
            c = count(lambda k, idx: (k == tau) & (idx <= mi + step - 1))
            return jnp.where(c < need, mi + step, mi)
        return lax.fori_loop(0, idx_bits, tie_bit, jnp.zeros((1, tq), jnp.int32))

    partial_ties = (n_ge > top_k) & (tau > KEY_NEG_INF)
    mi = lax.cond(jnp.max(partial_ties.astype(jnp.int32)) > 0, tie_search,
                  lambda: jnp.full((1, tq), seq_len, jnp.int32))

    def bias_block(j, _):
        k = key_scr[j]
        idx = j * tk + row_k
        sel = (k > tau) | ((k == tau) & (idx <= mi))
        adm = (idx // CHUNK) <= qchunk
        bias_scr[j] = jnp.where(sel & adm, 0.0, NEG)
        return 0

    lax.fori_loop(0, nkv, bias_block, 0)

    for h0 in range(0, DSA_HEADS, DSA_HEADS_PER_LOOP):
        heads = tuple(range(h0, h0 + DSA_HEADS_PER_LOOP))
        qs = [qd_ref[:, h * DSA_HEAD_DIM:(h + 1) * DSA_HEAD_DIM] for h in heads]

        def att(j, carries, heads=heads, qs=qs):
            off = pl.multiple_of(j * tk, tk)
            sls = [slice(h * DSA_HEAD_DIM, (h + 1) * DSA_HEAD_DIM) for h in heads]
            kbs = [kd_ref[pl.ds(off, tk), sl] for sl in sls]
            vtbs = [vt_ref[j, sl, :] for sl in sls]
            return _flash_t_steps(carries, qs, kbs, vtbs, bias_scr[j], s_scr)

        carries = lax.fori_loop(0, nkv, att, tuple(_flash_t_init(tq, DSA_HEAD_DIM) for _ in heads))
        for h, (_, l, acc) in zip(heads, carries):
            o_ref[:, h * DSA_HEAD_DIM:(h + 1) * DSA_HEAD_DIM] = (acc / l).T.astype(o_ref.dtype)


def _dsa_attn(qd, qi, wit, kd, vt, kia, kib, B, S, tq, tk, top_k):
    T = B * S
    nq = S // tq
    hd = DSA_HEADS * DSA_HEAD_DIM
    hi = IDX_HEADS * IDX_HEAD_DIM
    idx_bits = max(1, (S - 1).bit_length())
    assert vt.shape == (T // tk, hd, tk)
    qrow = lambda w: pl.BlockSpec((tq, w), lambda b, i: (b * nq + i, 0))
    seq = lambda w: pl.BlockSpec((S, w), lambda b, i: (b, 0))
    return pl.pallas_call(
        functools.partial(_dsa_attn_body, tq=tq, tk=tk, top_k=top_k, idx_bits=idx_bits, seq_len=S),
        grid=(B, nq),
        in_specs=[qrow(hd), qrow(hi), pl.BlockSpec((IDX_HEADS, tq), lambda b, i: (0, b * nq + i)), seq(hd),
                  pl.BlockSpec((S // tk, hd, tk), lambda b, i: (b, 0, 0)), seq(LANES), seq(LANES)],
        out_specs=qrow(hd),
        out_shape=jax.ShapeDtypeStruct((T, hd), jnp.bfloat16),
        scratch_shapes=[pltpu.VMEM((S // tk, tk, tq), jnp.int32), pltpu.VMEM((S // tk, tk, tq), jnp.float32),
                        pltpu.VMEM((DSA_HEADS_PER_LOOP, tk, tq), jnp.float32)],
        compiler_params=_cparams(("arbitrary", "arbitrary")),
        name="dsa_attn",
    )(qd, qi, wit, kd, vt, kia, kib)


def _merge_body(oa_ref, ob_ref, gc_ref, x_ref, woa_ref, wob_ref, wout_ref, bg_ref, gf_ref, wr_ref, br_ref,
                x1_ref, hn_ref, lg_ref):
    ya = jnp.dot(oa_ref[...], woa_ref[...], preferred_element_type=jnp.float32)
    yb = jnp.dot(ob_ref[...], wob_ref[...], preferred_element_type=jnp.float32)
    g = jax.nn.sigmoid(gc_ref[...].astype(jnp.float32) + bg_ref[...])
    merged = g[:, :D_MODEL] * ya + g[:, D_MODEL:] * yb
    x1 = x_ref[...] + jnp.dot(merged.astype(jnp.bfloat16), wout_ref[...], preferred_element_type=jnp.float32)
    x1_ref[...] = x1
    ms = jnp.mean(x1 * x1, axis=-1, keepdims=True)
    hn = x1 * lax.rsqrt(ms + NORM_EPS) * gf_ref[...]
    hn_ref[...] = hn.reshape(hn_ref.shape)
    lg_ref[...] = lax.dot_general(wr_ref[...], hn.astype(jnp.bfloat16), _NT,
                                  preferred_element_type=jnp.float32) + br_ref[...]


def _merge(oa, ob, proj, x2, woa, wob, wout, bg, gf, wr_t, br, tm):
    T = x2.shape[0]
    row = lambda w: pl.BlockSpec((tm, w), lambda i: (i, 0))
    full = lambda shape: pl.BlockSpec(shape, lambda i: (0, 0))
    return pl.pallas_call(
        _merge_body,
        grid=(T // tm,),
        in_specs=[row(D_MODEL), row(D_MODEL), row(2 * D_MODEL), row(D_MODEL),
                  full((D_MODEL, D_MODEL)), full((D_MODEL, D_MODEL)), full((D_MODEL, D_MODEL)),
                  full((1, 2 * D_MODEL)), full((1, D_MODEL)), full((N_EXPERTS, D_MODEL)), full((N_EXPERTS, 1))],
        out_specs=[row(D_MODEL), pl.BlockSpec((tm,) + ROW_TILE, lambda i: (i, 0, 0)),
                   pl.BlockSpec((N_EXPERTS, tm), lambda i: (0, i))],
        out_shape=[
            jax.ShapeDtypeStruct((T, D_MODEL), jnp.float32),
            jax.ShapeDtypeStruct((T,) + ROW_TILE, jnp.float32),
            jax.ShapeDtypeStruct((N_EXPERTS, T), jnp.float32),
        ],
        compiler_params=_cparams(("arbitrary",)),
        name="merge",
    )(oa, ob, proj, x2, woa, wob, wout, bg, gf, wr_t, br)


def _route_body(lg_ref, eidx_ref, gate_ref, rank_ref, cnt_ref, carry_scr, *, tr):
    @pl.when(pl.program_id(0) == 0)
    def _():
        carry_scr[...] = jnp.zeros_like(carry_scr)

    vals = lg_ref[...]
    rows = lax.broadcasted_iota(jnp.int32, vals.shape, 0)
    tops, hots = [], []
    for k in range(MOE_TOP_K):
        m = jnp.max(vals, axis=0, keepdims=True)
        idx = jnp.min(jnp.where(vals == m, rows, N_EXPERTS), axis=0, keepdims=True)
        hot = rows == idx
        vals = jnp.where(hot, -jnp.inf, vals)
        tops.append(m)
        hots.append(hot)
        eidx_ref[k:k + 1, :] = idx
    es = [jnp.exp(m - tops[0]) for m in tops]
    den = es[0] + es[1] + es[2] + es[3]
    for k in range(MOE_TOP_K):
        gate_ref[k:k + 1, :] = es[k] / den
    sel = jnp.zeros(vals.shape, jnp.float32)
    for hot in hots:
        sel = sel + hot.astype(jnp.float32)
    before = (lax.broadcasted_iota(jnp.int32, (tr, tr), 0) < lax.broadcasted_iota(jnp.int32, (tr, tr), 1))
    prefix = jnp.dot(sel.astype(jnp.bfloat16), before.astype(jnp.bfloat16), preferred_element_type=jnp.float32)
    prefix = prefix + carry_scr[:, 0:1]
    for k in range(MOE_TOP_K):
        rank_ref[k:k + 1, :] = jnp.sum(jnp.where(hots[k], prefix, 0.0), axis=0, keepdims=True).astype(jnp.int32)
    carry_scr[...] = carry_scr[...] + jnp.sum(sel, axis=1, keepdims=True)
    cnt_ref[...] = carry_scr[...]


def _route(logits_t, tr):
    T = logits_t.shape[1]
    tok = pl.BlockSpec((MOE_TOP_K, tr), lambda i: (0, i))
    return pl.pallas_call(
        functools.partial(_route_body, tr=tr),
        grid=(T // tr,),
        in_specs=[pl.BlockSpec((N_EXPERTS, tr), lambda i: (0, i))],
        out_specs=[tok, tok, tok, pl.BlockSpec((N_EXPERTS, LANES), lambda i: (0, 0))],
        out_shape=[
            jax.ShapeDtypeStruct((MOE_TOP_K, T), jnp.int32),
            jax.ShapeDtypeStruct((MOE_TOP_K, T), jnp.float32),
            jax.ShapeDtypeStruct((MOE_TOP_K, T), jnp.int32),
            jax.ShapeDtypeStruct((N_EXPERTS, LANES), jnp.float32),
        ],
        scratch_shapes=[pltpu.VMEM((N_EXPERTS, LANES), jnp.float32)],
        compiler_params=_cparams(("arbitrary",)),
        name="route",
    )(logits_t)


def _dest_body(poff_ref, eidx_ref, rank_ref, dest_ref):
    e = eidx_ref[...]
    base = jnp.zeros(e.shape, jnp.int32)
    for x in range(N_EXPERTS):
        base = jnp.where(e == x, poff_ref[x], base)
    dest_ref[...] = base + rank_ref[...]


def _dest(pad_off, eidx, rank, tr):
    T = eidx.shape[1]
    tok = pl.BlockSpec((MOE_TOP_K, tr), lambda i, p: (0, i))
    return pl.pallas_call(
        _dest_body,
        grid_spec=pltpu.PrefetchScalarGridSpec(
            num_scalar_prefetch=1, grid=(T // tr,), in_specs=[tok, tok], out_specs=tok),
        out_shape=jax.ShapeDtypeStruct((MOE_TOP_K, T), jnp.int32),
        compiler_params=_cparams(("arbitrary",)),
        name="dest",
    )(pad_off, eidx, rank)


def _dispatch_body(zstart_ref, zvalid_ref, dest_ref, dest_prev_ref, hn_ref, xs_ref, z_scr, zsem, hbuf, lsems, rsems,
                   *, tm):
    @pl.when(pl.program_id(0) == 0)
    def _():
        z_scr[...] = jnp.zeros_like(z_scr)

        def zero_copy(n):
            return pltpu.make_async_copy(z_scr, xs_ref.at[pl.ds(zstart_ref[n], MOE_BLOCK)], zsem)

        def zissue(n, _):
            @pl.when(zvalid_ref[n] > 0)
            def _():
                zero_copy(n).start()
            return 0

        def zdrain(n, _):
            @pl.when(zvalid_ref[n] > 0)
            def _():
                zero_copy(n).wait()
            return 0

        lax.fori_loop(0, 2 * N_EXPERTS, zissue, 0)
        lax.fori_loop(0, 2 * N_EXPERTS, zdrain, 0)

    i = pl.program_id(0)
    n = pl.num_programs(0)

    def tile_load(step):
        s = step % DISPATCH_SLOTS
        return pltpu.make_async_copy(hn_ref.at[pl.ds(step * tm, tm)], hbuf.at[s], lsems.at[s])

    def row_copy(dref, s, k, t):
        return pltpu.make_async_copy(hbuf.at[s, t], xs_ref.at[dref[k, t]], rsems.at[s])

    def drain_rows(dref, s):
        def drain(t, _):
            for k in range(MOE_TOP_K):
                row_copy(dref, s, k, t).wait()
            return 0
        lax.fori_loop(0, tm, drain, 0, unroll=DRAIN_UNROLL)

    @pl.when(i == 0)
    def _():
        tile_load(0).start()

        @pl.when(n > 1)
        def _():
            tile_load(1).start()

    slot = i % DISPATCH_SLOTS
    tile_load(i).wait()

    def issue(t, _):
        for k in range(MOE_TOP_K):
            row_copy(dest_ref, slot, k, t).start(priority=k % DMA_PRIORITIES)
        return 0

    lax.fori_loop(0, tm, issue, 0)

    @pl.when(i > 0)
    def _():
        drain_rows(dest_prev_ref, (i - 1) % DISPATCH_SLOTS)

    @pl.when(i + 2 < n)
    def _():
        tile_load(i + 2).start()

    @pl.when(i == n - 1)
    def _():
        drain_rows(dest_ref, slot)


def _dispatch(zero_start, zero_valid, dest, hn, P, tm):
    T = hn.shape[0]
    return pl.pallas_call(
        functools.partial(_dispatch_body, tm=tm),
        grid_spec=pltpu.PrefetchScalarGridSpec(
            num_scalar_prefetch=2, grid=(T // tm,),
            in_specs=[
                pl.BlockSpec((MOE_TOP_K, tm), lambda i, zs, zv: (0, i), memory_space=pltpu.SMEM),
                pl.BlockSpec((MOE_TOP_K, tm), lambda i, zs, zv: (0, jnp.maximum(i - 1, 0)), memory_space=pltpu.SMEM),
                pl.BlockSpec(memory_space=pl.ANY),
            ],
            out_specs=pl.BlockSpec(memory_space=pl.ANY),
            scratch_shapes=[pltpu.VMEM((MOE_BLOCK,) + ROW_TILE, hn.dtype), pltpu.SemaphoreType.DMA(()),
                            pltpu.VMEM((DISPATCH_SLOTS, tm) + ROW_TILE, hn.dtype),
                            pltpu.SemaphoreType.DMA((DISPATCH_SLOTS,)), pltpu.SemaphoreType.DMA((DISPATCH_SLOTS,))]),
        out_shape=jax.ShapeDtypeStruct((P,) + ROW_TILE, hn.dtype),
        compiler_params=_cparams(("arbitrary",)),
        name="dispatch",
    )(zero_start, zero_valid, dest, dest, hn)


def _experts_body(bexp_ref, first_ref, nused_ref, xs_ref, wgu_ref, bgu_ref, wd_ref, bd_ref, y_ref, wgu_s, wd_s):
    j = pl.program_id(0)

    @pl.when(first_ref[j] > 0)
    def _():
        wgu_s[...] = wgu_ref[0].astype(jnp.bfloat16)
        wd_s[...] = wd_ref[0].astype(jnp.bfloat16)

    @pl.when(j < nused_ref[0])
    def _():
        xb = xs_ref[...].reshape(MOE_BLOCK, D_MODEL).astype(jnp.bfloat16)
        h = jnp.dot(xb, wgu_s[...], preferred_element_type=jnp.float32) + bgu_ref[0]
        gate = jnp.minimum(h[:, :D_EXPERT], SWIGLU_LIMIT)
        lin = jnp.clip(h[:, D_EXPERT:], -SWIGLU_LIMIT, SWIGLU_LIMIT)
        glu = gate * jax.nn.sigmoid(SWIGLU_ALPHA * gate)
        act = ((lin + 1.0) * glu).astype(jnp.bfloat16)
        y = jnp.dot(act, wd_s[...], preferred_element_type=jnp.float32) + bd_ref[0]
        y_ref[...] = y.reshape(y_ref.shape)

    @pl.when(j >= nused_ref[0])
    def _():
        y_ref[...] = jnp.zeros_like(y_ref)


def _experts(blk_expert, blk_first, n_used, xs, wgu, bgu, wd, bd):
    P = xs.shape[0]
    nb = P // MOE_BLOCK
    rows = pl.BlockSpec((MOE_BLOCK,) + ROW_TILE, lambda j, be, bf, nu: (j, 0, 0))
    return pl.pallas_call(
        _experts_body,
        grid_spec=pltpu.PrefetchScalarGridSpec(
            num_scalar_prefetch=3, grid=(nb,),
            in_specs=[
                rows,
                pl.BlockSpec((1, D_MODEL, 2 * D_EXPERT), lambda j, be, bf, nu: (be[j], 0, 0)),
                pl.BlockSpec((1, 1, 2 * D_EXPERT), lambda j, be, bf, nu: (be[j], 0, 0)),
                pl.BlockSpec((1, D_EXPERT, D_MODEL), lambda j, be, bf, nu: (be[j], 0, 0)),
                pl.BlockSpec((1, 1, D_MODEL), lambda j, be, bf, nu: (be[j], 0, 0)),
            ],
            out_specs=rows,
            scratch_shapes=[pltpu.VMEM((D_MODEL, 2 * D_EXPERT), jnp.bfloat16),
                            pltpu.VMEM((D_EXPERT, D_MODEL), jnp.bfloat16)]),
        out_shape=jax.ShapeDtypeStruct((P,) + ROW_TILE, jnp.float32),
        compiler_params=_cparams(("arbitrary",)),
        name="experts",
    )(blk_expert, blk_first, n_used, xs, wgu, bgu, wd, bd)


def _combine_body(dest_ref, dest_next_ref, y_ref, x1_ref, gate_ref, gn_ref, o_ref, ybuf, sems, *, tm):
    i = pl.program_id(0)
    slot = i % 2

    def row_copy(dref, s, k, t):
        return pltpu.make_async_copy(y_ref.at[dref[k, t]], ybuf.at[s, k, t], sems.at[s])

    def issue_rows(dref, s):
        def issue(t, _):
            for k in range(MOE_TOP_K):
                row_copy(dref, s, k, t).start(priority=k % DMA_PRIORITIES)
            return 0
        lax.fori_loop(0, tm, issue, 0)

    @pl.when(i == 0)
    def _():
        issue_rows(dest_ref, 0)

    @pl.when(i + 1 < pl.num_programs(0))
    def _():
        issue_rows(dest_next_ref, 1 - slot)

    def drain(t, _):
        for k in range(MOE_TOP_K):
            row_copy(dest_ref, slot, k, t).wait()
        return 0

    lax.fori_loop(0, tm, drain, 0, unroll=DRAIN_UNROLL)

    g = gate_ref[...]
    acc = x1_ref[...]
    for k in range(MOE_TOP_K):
        acc = acc + g[:, k:k + 1] * ybuf[slot, k].reshape(tm, D_MODEL)
    ms = jnp.mean(acc * acc, axis=-1, keepdims=True)
    o_ref[...] = acc * lax.rsqrt(ms + NORM_EPS) * gn_ref[...]


def _combine(dest, y, x1, gate_t, gn, tm):
    T = x1.shape[0]
    n = T // tm
    return pl.pallas_call(
        functools.partial(_combine_body, tm=tm),
        grid=(n,),
        in_specs=[
            pl.BlockSpec((MOE_TOP_K, tm), lambda i: (0, i), memory_space=pltpu.SMEM),
            pl.BlockSpec((MOE_TOP_K, tm), lambda i: (0, jnp.minimum(i + 1, n - 1)), memory_space=pltpu.SMEM),
            pl.BlockSpec(memory_space=pl.ANY),
            pl.BlockSpec((tm, D_MODEL), lambda i: (i, 0)),
            pl.BlockSpec((tm, MOE_TOP_K), lambda i: (i, 0)),
            pl.BlockSpec((1, D_MODEL), lambda i: (0, 0)),
        ],
        out_specs=pl.BlockSpec((tm, D_MODEL), lambda i: (i, 0)),
        out_shape=jax.ShapeDtypeStruct((T, D_MODEL), jnp.float32),
        scratch_shapes=[pltpu.VMEM((2, MOE_TOP_K, tm) + ROW_TILE, jnp.float32), pltpu.SemaphoreType.DMA((2,))],
        compiler_params=_cparams(("arbitrary",)),
        name="combine",
    )(dest, dest, y, x1, gate_t, gn)


def _rope_tables(S):
    pos = jnp.arange(S, dtype=jnp.float32)[:, None]

    def cs(half):
        inv = ROPE_THETA ** (-jnp.arange(half, dtype=jnp.float32) / half)
        ang = pos * inv[None, :]
        return jnp.cos(ang), jnp.sin(ang)

    z = lambda n: jnp.zeros((S, n), jnp.float32)
    o = lambda n: jnp.ones((S, n), jnp.float32)
    cat = lambda *a: jnp.concatenate(a, axis=1)
    c, s = cs(MLA_ROPE // 2)
    mla = (cat(c, c, z(64)), cat(z(32), s, z(64)), cat(-s, z(96)))
    c, s = cs(DSA_HEAD_DIM // 2)
    dsa = (cat(c, c), cat(-s, s))
    c, s = cs(IDX_ROPE // 2)
    c1, sa1, sb1 = cat(c, c, o(32)), cat(z(16), s, z(32)), cat(-s, z(48))
    idx_q = (cat(c1, c1), cat(sa1, sa1), cat(sb1, sb1))
    idx_k = (cat(c1, o(64)), cat(sa1, z(64)), cat(sb1, z(64)))
    return mla, dsa + idx_q + idx_k


def _pack_w_in(w):
    sizes = (MLA_Q_LORA, MLA_KV_LORA, MLA_ROPE, 3 * DSA_HEADS * DSA_HEAD_DIM, IDX_HEADS * IDX_HEAD_DIM,
             IDX_HEAD_DIM, IDX_HEADS, 2 * D_MODEL)
    pts, acc = [], 0
    for s in sizes[:-1]:
        acc += s
        pts.append(acc)
    c_q, c_kv, k_r, qkv, q_idx, k_idx, w_idx, gates = jnp.split(w, pts, axis=1)
    z = lambda n: jnp.zeros((w.shape[0], n), w.dtype)
    packed = jnp.concatenate([gates, qkv, q_idx, k_idx, w_idx, z(LANES - IDX_HEAD_DIM - IDX_HEADS),
                              c_q, c_kv, k_r, z(LANES - MLA_ROPE)], axis=1)
    assert packed.shape[1] == D_IN_PACKED
    return packed.astype(jnp.bfloat16)


def _pack_w_uq(w):
    w = w.reshape(MLA_Q_LORA, MLA_HEADS, MLA_QK)
    w = jnp.pad(w, ((0, 0), (0, 0), (0, MLA_QK_PAD - MLA_QK)))
    return w.reshape(MLA_Q_LORA, MLA_HEADS * MLA_QK_PAD).astype(jnp.bfloat16)


def _tile(n, pref):
    t = min(n, pref)
    assert n % t == 0
    return t


def kernel(x, norm_attn_g, w_in, q_lora_g, kv_lora_g, w_uq, w_uk, w_uv, w_o_mla, w_o_dsa, b_gate, w_out,
           norm_ffn_g, w_router, b_router, w_gate_up, b_gate_up, w_down, b_down, norm_final_g):
    B, S, D = x.shape
    assert D == D_MODEL and S % CHUNK == 0 and norm_attn_g.shape[0] == 1
    T = B * S
    top_k = min(DSA_TOPK_MAX, S // 4)
    bf = jnp.bfloat16
    x2 = x.reshape(T, D)
    mla_tabs, dsa_tabs = _rope_tables(S)

    proj = _in_proj(x2, norm_attn_g, _pack_w_in(w_in[0]), _tile(T, 1024), D_IN_PACKED // 3)
    tk = _tile(S, 512)
    tq = _tile(S, 512)
    q, k, vt = _mla_prep(proj, q_lora_g, kv_lora_g, _pack_w_uq(w_uq[0]), w_uk[0].astype(bf), w_uv[0].T.astype(bf),
                         mla_tabs, S, tk)
    o_mla = _mla_attn(q, k, vt, B, S, tq, tk)
    qd, kd, vtd, qi, kia, kib, wit = _dsa_prep(proj, dsa_tabs, S, tk)
    o_dsa = _dsa_attn(qd, qi, wit, kd, vtd, kia, kib, B, S, tq, tk, top_k)
    x1, hn, logits_t = _merge(o_mla, o_dsa, proj, x2, w_o_mla[0].astype(bf), w_o_dsa[0].astype(bf),
                              w_out[0].astype(bf), b_gate, norm_ffn_g, w_router[0].T.astype(bf),
                              b_router[0][:, None], _tile(T, 512))

    tr = _tile(T, 512)
    eidx, gate, rank, cnt = _route(logits_t, tr)
    counts = cnt[:, 0].astype(jnp.int32)
    padded = ((counts + MOE_BLOCK - 1) // MOE_BLOCK) * MOE_BLOCK
    pad_end = jnp.cumsum(padded)
    pad_off = pad_end - padded
    P = T * MOE_TOP_K + N_EXPERTS * MOE_BLOCK
    nb = P // MOE_BLOCK
    blk_start = jnp.arange(nb, dtype=jnp.int32) * MOE_BLOCK
    blk_expert = jnp.minimum(jnp.sum(pad_end[None, :] <= blk_start[:, None], axis=1), N_EXPERTS - 1).astype(jnp.int32)
    blk_first = jnp.concatenate([jnp.ones((1,), jnp.int32), (blk_expert[1:] != blk_expert[:-1]).astype(jnp.int32)])
    n_used = (pad_end[-1:] // MOE_BLOCK).astype(jnp.int32)

    dest = _dest(pad_off.astype(jnp.int32), eidx, rank, tr)
    trailing = n_used[0] + jnp.arange(N_EXPERTS, dtype=jnp.int32)
    zero_start = jnp.concatenate([pad_end - MOE_BLOCK, jnp.minimum(trailing, nb - 1) * MOE_BLOCK]).astype(jnp.int32)
    zero_valid = jnp.concatenate([padded > 0, trailing < nb]).astype(jnp.int32)
    xs = _dispatch(zero_start, zero_valid, dest, hn, P, _tile(T, 128))
    y = _experts(blk_expert, blk_first, n_used, xs, w_gate_up[0], b_gate_up[0][:, None, :],
                 w_down[0], b_down[0][:, None, :])
    out = _combine(dest, y, x1, gate.T, norm_final_g[None, :], _tile(T, 128))
    return out.reshape(B, S, D)
```

```python
import functools

import jax
import jax.numpy as jnp
from jax import lax
from jax.experimental import pallas as pl
from jax.experimental.pallas import tpu as pltpu

D_MODEL = 1024
CHUNK = 64
ROPE_THETA = 10000.0
NORM_EPS = 1e-6

MLA_HEADS = 8
MLA_Q_LORA = 384
MLA_KV_LORA = 256
MLA_NOPE = 128
MLA_ROPE = 64
MLA_V = 128
MLA_QK = MLA_NOPE + MLA_ROPE
MLA_QK_PAD = 256

DSA_HEADS = 8
DSA_HEAD_DIM = 128
IDX_HEADS = 8
IDX_HEAD_DIM = 64
IDX_ROPE = 32
DSA_TOPK_MAX = 256

N_EXPERTS = 32
MOE_TOP_K = 4
D_EXPERT = 1024
SWIGLU_LIMIT = 7.0
SWIGLU_ALPHA = 1.702

LANES = 128
HEADS_PER_LOOP = 8
DSA_HEADS_PER_LOOP = 4
MOE_BLOCK = 512
DISPATCH_SLOTS = 3
DMA_PRIORITIES = 2
ISSUE_UNROLL = 4
DRAIN_UNROLL = 8
ROW_TILE = (D_MODEL // LANES, LANES)
VMEM_LIMIT = 56 * 1024 * 1024

OFF_GATES = 0
OFF_Q = 2048
OFF_K = 3072
OFF_V = 4096
OFF_QIDX = 5120
OFF_KIDX = 5632
OFF_CQ = 5760
OFF_CKV = 6144
OFF_KR = 6400
D_IN_PACKED = 6528

LOG2E = 1.4426950408889634
NEG = -1e30
INT_MIN = -(2 ** 31)
KEY_NEG_INF = -2139095041

_NT = (((1,), (1,)), ((), ()))


def _cparams(sem):
    return pltpu.CompilerParams(dimension_semantics=sem, vmem_limit_bytes=VMEM_LIMIT)


def _rope3(x, c, sa, sb, shift):
    n = x.shape[-1]
    return x * c + pltpu.roll(x, shift, 1) * sa + pltpu.roll(x, n - shift, 1) * sb


def _inproj_body(x_ref, g_ref, w_ref, o_ref, h_scr):
    @pl.when(pl.program_id(1) == 0)
    def _():
        x = x_ref[...]
        ms = jnp.mean(x * x, axis=-1, keepdims=True)
        h_scr[...] = (x * lax.rsqrt(ms + NORM_EPS) * g_ref[...]).astype(jnp.bfloat16)

    o_ref[...] = jnp.dot(h_scr[...], w_ref[...], preferred_element_type=jnp.float32).astype(o_ref.dtype)


def _in_proj(x2, g, w_packed, tm, tn):
    T = x2.shape[0]
    return pl.pallas_call(
        _inproj_body,
        grid=(T // tm, D_IN_PACKED // tn),
        in_specs=[
            pl.BlockSpec((tm, D_MODEL), lambda i, j: (i, 0)),
            pl.BlockSpec((1, D_MODEL), lambda i, j: (0, 0)),
            pl.BlockSpec((D_MODEL, tn), lambda i, j: (0, j)),
        ],
        out_specs=pl.BlockSpec((tm, tn), lambda i, j: (i, j)),
        out_shape=jax.ShapeDtypeStruct((T, D_IN_PACKED), jnp.bfloat16),
        scratch_shapes=[pltpu.VMEM((tm, D_MODEL), jnp.bfloat16)],
        compiler_params=_cparams(("arbitrary", "arbitrary")),
        name="in_proj",
    )(x2, g, w_packed)


def _mla_prep_body(cq_ref, ckv_ref, kr_ref, gq_ref, gkv_ref, wq_ref, wk_ref, wv_ref,
                   c_ref, sa_ref, sb_ref, q_ref, k_ref, v_ref):
    def norm(ref, g_ref):
        x = ref[...].astype(jnp.float32)
        ms = jnp.mean(x * x, axis=-1, keepdims=True)
        return (x * lax.rsqrt(ms + NORM_EPS) * g_ref[...]).astype(jnp.bfloat16)

    cq = norm(cq_ref, gq_ref)
    ckv = norm(ckv_ref, gkv_ref)
    c, sa, sb = c_ref[...], sa_ref[...], sb_ref[...]
    scale = MLA_QK ** -0.5 * LOG2E
    q_all = jnp.dot(cq, wq_ref[...], preferred_element_type=jnp.float32)
    kn = jnp.dot(ckv, wk_ref[...], preferred_element_type=jnp.float32)
    v_ref[0] = lax.dot_general(wv_ref[...], ckv, _NT, preferred_element_type=jnp.float32).astype(v_ref.dtype)
    kr = _rope3(kr_ref[...].astype(jnp.float32), c, sa, sb, MLA_ROPE // 2).astype(k_ref.dtype)
    for h in range(MLA_HEADS):
        o = h * MLA_QK_PAD
        q_ref[:, o:o + LANES] = (q_all[:, o:o + LANES] * scale).astype(q_ref.dtype)
        qr = _rope3(q_all[:, o + LANES:o + 2 * LANES], c, sa, sb, MLA_ROPE // 2)
        q_ref[:, o + LANES:o + 2 * LANES] = (qr * scale).astype(q_ref.dtype)
        k_ref[:, o:o + LANES] = kn[:, h * LANES:(h + 1) * LANES].astype(k_ref.dtype)
        k_ref[:, o + LANES:o + 2 * LANES] = kr


def _mla_prep(proj, gq, gkv, wq, wk, wv, tabs, S, tm):
    T = proj.shape[0]
    ns = S // tm
    full = lambda shape: pl.BlockSpec(shape, lambda i: (0, 0))
    tab = pl.BlockSpec((tm, LANES), lambda i: (i % ns, 0))
    hq = MLA_HEADS * MLA_QK_PAD
    return pl.pallas_call(
        _mla_prep_body,
        grid=(T // tm,),
        in_specs=[
            pl.BlockSpec((tm, MLA_Q_LORA), lambda i: (i, OFF_CQ // MLA_Q_LORA)),
            pl.BlockSpec((tm, MLA_KV_LORA), lambda i: (i, OFF_CKV // MLA_KV_LORA)),
            pl.BlockSpec((tm, LANES), lambda i: (i, OFF_KR // LANES)),
            full((1, MLA_Q_LORA)), full((1, MLA_KV_LORA)),
            full((MLA_Q_LORA, hq)), full((MLA_KV_LORA, MLA_HEADS * MLA_NOPE)),
            full((MLA_HEADS * MLA_V, MLA_KV_LORA)),
            tab, tab, tab,
        ],
        out_specs=[
            pl.BlockSpec((tm, hq), lambda i: (i, 0)),
            pl.BlockSpec((tm, hq), lambda i: (i, 0)),
            pl.BlockSpec((1, MLA_HEADS * MLA_V, tm), lambda i: (i, 0, 0)),
        ],
        out_shape=[
            jax.ShapeDtypeStruct((T, hq), jnp.bfloat16),
            jax.ShapeDtypeStruct((T, hq), jnp.bfloat16),
            jax.ShapeDtypeStruct((T // tm, MLA_HEADS * MLA_V, tm), jnp.bfloat16),
        ],
        compiler_params=_cparams(("arbitrary",)),
        name="mla_prep",
    )(proj, proj, proj, gq, gkv, wq, wk, wv, *tabs)


def _flash_t_init(tq, dv):
    return (jnp.full((1, tq), NEG, jnp.float32), jnp.zeros((1, tq), jnp.float32), jnp.zeros((dv, tq), jnp.float32))


def _flash_t_steps(carries, qs, kbs, vtbs, bias, s_scr):
    for n, (qh, kb) in enumerate(zip(qs, kbs)):
        s_scr[n] = lax.dot_general(kb, qh, _NT, preferred_element_type=jnp.float32)
    out = []
    for n, ((m, l, acc), vtb) in enumerate(zip(carries, vtbs)):
        s = s_scr[n]
        if bias is not None:
            s = s + bias
        m_new = jnp.maximum(m, jnp.max(s, axis=0, keepdims=True))
        alpha = jnp.exp2(m - m_new)
        p = jnp.exp2(s - m_new)
        l = alpha * l + jnp.sum(p, axis=0, keepdims=True)
        acc = alpha * acc + jnp.dot(vtb, p.astype(vtb.dtype), preferred_element_type=jnp.float32)
        out.append((m_new, l, acc))
    return tuple(out)


def _mla_attn_body(q_ref, k_ref, vt_ref, o_ref, s_scr, *, tq, tk):
    i = pl.program_id(1)
    q0 = i * tq
    nkv = (q0 + tq + tk - 1) // tk
    last = nkv - 1
    kc = (last * tk + lax.broadcasted_iota(jnp.int32, (tk, tq), 0)) // CHUNK
    qc = (q0 + lax.broadcasted_iota(jnp.int32, (tk, tq), 1)) // CHUNK
    last_bias = jnp.where(kc <= qc, 0.0, NEG)

    for h0 in range(0, MLA_HEADS, HEADS_PER_LOOP):
        heads = tuple(range(h0, h0 + HEADS_PER_LOOP))
        qs = [q_ref[:, h * MLA_QK_PAD:(h + 1) * MLA_QK_PAD] for h in heads]

        def step(j, carries, bias, heads=heads, qs=qs):
            off = pl.multiple_of(j * tk, tk)
            kbs = [k_ref[pl.ds(off, tk), h * MLA_QK_PAD:(h + 1) * MLA_QK_PAD] for h in heads]
            vtbs = [vt_ref[j, h * MLA_V:(h + 1) * MLA_V, :] for h in heads]
            return _flash_t_steps(carries, qs, kbs, vtbs, bias, s_scr)

        init = tuple(_flash_t_init(tq, MLA_V) for _ in heads)
        carries = lax.fori_loop(0, last, lambda j, c: step(j, c, None), init)
        carries = step(last, carries, last_bias)
        for h, (_, l, acc) in zip(heads, carries):
            o_ref[:, h * MLA_V:(h + 1) * MLA_V] = (acc / l).T.astype(o_ref.dtype)


def _mla_attn(q, k, vt, B, S, tq, tk):
    T = B * S
    nq = S // tq
    hq = MLA_HEADS * MLA_QK_PAD
    hv = MLA_HEADS * MLA_V
    assert vt.shape == (T // tk, hv, tk)
    return pl.pallas_call(
        functools.partial(_mla_attn_body, tq=tq, tk=tk),
        grid=(B, nq),
        in_specs=[
            pl.BlockSpec((tq, hq), lambda b, i: (b * nq + i, 0)),
            pl.BlockSpec((S, hq), lambda b, i: (b, 0)),
            pl.BlockSpec((S // tk, hv, tk), lambda b, i: (b, 0, 0)),
        ],
        out_specs=pl.BlockSpec((tq, hv), lambda b, i: (b * nq + i, 0)),
        out_shape=jax.ShapeDtypeStruct((T, hv), jnp.bfloat16),
        scratch_shapes=[pltpu.VMEM((HEADS_PER_LOOP, tk, tq), jnp.float32)],
        compiler_params=_cparams(("arbitrary", "arbitrary")),
        name="mla_attn",
    )(q, k, vt)


def _dsa_prep_body(q_ref, k_ref, v_ref, qi_ref, kx_ref, cd_ref, sd_ref, ci_ref, sai_ref, sbi_ref,
                   ck_ref, sak_ref, sbk_ref, qo_ref, ko_ref, vt_ref, qio_ref, kia_ref, kib_ref, wit_ref):
    cd, sd = cd_ref[...], sd_ref[...]
    for h in range(DSA_HEADS):
        sl = slice(h * LANES, (h + 1) * LANES)
        xq = q_ref[:, sl].astype(jnp.float32)
        xk = k_ref[:, sl].astype(jnp.float32)
        rq = xq * cd + pltpu.roll(xq, DSA_HEAD_DIM // 2, 1) * sd
        rk = xk * cd + pltpu.roll(xk, DSA_HEAD_DIM // 2, 1) * sd
        qo_ref[:, sl] = (rq * (DSA_HEAD_DIM ** -0.5 * LOG2E)).astype(qo_ref.dtype)
        ko_ref[:, sl] = rk.astype(ko_ref.dtype)
        vt_ref[0, sl, :] = v_ref[:, sl].astype(jnp.float32).T.astype(vt_ref.dtype)
    ci, sai, sbi = ci_ref[...], sai_ref[...], sbi_ref[...]
    for p in range(IDX_HEADS * IDX_HEAD_DIM // LANES):
        sl = slice(p * LANES, (p + 1) * LANES)
        r = _rope3(qi_ref[:, sl].astype(jnp.float32), ci, sai, sbi, IDX_ROPE // 2)
        qio_ref[:, sl] = (r * (IDX_HEAD_DIM ** -0.5)).astype(qio_ref.dtype)
    kx = kx_ref[...].astype(jnp.float32)
    kr = _rope3(kx, ck_ref[...], sak_ref[...], sbk_ref[...], IDX_ROPE // 2)
    lane = lax.broadcasted_iota(jnp.int32, kr.shape, 1)
    ka = jnp.where(lane < IDX_HEAD_DIM, kr, 0.0)
    kia_ref[...] = ka.astype(kia_ref.dtype)
    kib_ref[...] = pltpu.roll(ka, IDX_HEAD_DIM, 1).astype(kib_ref.dtype)
    wit_ref[...] = kx.T[IDX_HEAD_DIM:IDX_HEAD_DIM + IDX_HEADS, :] * (IDX_HEADS ** -0.5)


def _dsa_prep(proj, tabs, S, tm):
    T = proj.shape[0]
    ns = S // tm
    tab = pl.BlockSpec((tm, LANES), lambda i: (i % ns, 0))
    hd = DSA_HEADS * DSA_HEAD_DIM
    hi = IDX_HEADS * IDX_HEAD_DIM
    row = lambda w, off: pl.BlockSpec((tm, w), lambda i: (i, off // w))
    out = lambda w: pl.BlockSpec((tm, w), lambda i: (i, 0))
    return pl.pallas_call(
        _dsa_prep_body,
        grid=(T // tm,),
        in_specs=[row(hd, OFF_Q), row(hd, OFF_K), row(hd, OFF_V), row(hi, OFF_QIDX), row(LANES, OFF_KIDX)] + [tab] * 8,
        out_specs=[out(hd), out(hd), pl.BlockSpec((1, hd, tm), lambda i: (i, 0, 0)), out(hi), out(LANES), out(LANES),
                   pl.BlockSpec((IDX_HEADS, tm), lambda i: (0, i))],
        out_shape=[
            jax.ShapeDtypeStruct((T, hd), jnp.bfloat16),
            jax.ShapeDtypeStruct((T, hd), jnp.bfloat16),
            jax.ShapeDtypeStruct((T // tm, hd, tm), jnp.bfloat16),
            jax.ShapeDtypeStruct((T, hi), jnp.bfloat16),
            jax.ShapeDtypeStruct((T, LANES), jnp.bfloat16),
            jax.ShapeDtypeStruct((T, LANES), jnp.bfloat16),
            jax.ShapeDtypeStruct((IDX_HEADS, T), jnp.float32),
        ],
        compiler_params=_cparams(("arbitrary",)),
        name="dsa_prep",
    )(proj, proj, proj, proj, proj, *tabs)


def _dsa_attn_body(qd_ref, qi_ref, wit_ref, kd_ref, vt_ref, kia_ref, kib_ref, o_ref, key_scr, bias_scr, s_scr,
                   *, tq, tk, top_k, idx_bits, seq_len):
    i = pl.program_id(1)
    q0 = i * tq
    nkv = (q0 + tq + tk - 1) // tk
    qchunk = (q0 + lax.broadcasted_iota(jnp.int32, (1, tq), 1)) // CHUNK
    row_k = lax.broadcasted_iota(jnp.int32, (tk, tq), 0)
    wit = wit_ref[...]

    def idx_block(j, _):
        off = pl.multiple_of(j * tk, tk)
        ka = kia_ref[pl.ds(off, tk), :]
        kb = kib_ref[pl.ds(off, tk), :]
        score = jnp.zeros((tk, tq), jnp.float32)
        for p in range(IDX_HEADS // 2):
            qp = qi_ref[:, p * LANES:(p + 1) * LANES]
            ra = lax.dot_general(ka, qp, _NT, preferred_element_type=jnp.float32)
            rb = lax.dot_general(kb, qp, _NT, preferred_element_type=jnp.float32)
            score = score + wit[2 * p:2 * p + 1, :] * jnp.maximum(ra, 0.0)
            score = score + wit[2 * p + 1:2 * p + 2, :] * jnp.maximum(rb, 0.0)
        adm = ((off + row_k) // CHUNK) <= qchunk
        score = jnp.where(adm, score + 0.0, -jnp.inf)
        b = pltpu.bitcast(score, jnp.int32)
        key_scr[j] = b ^ ((b >> 31) & 0x7FFFFFFF)
        return 0

    lax.fori_loop(0, nkv, idx_block, 0)

    def count(pred):
        def blk(j, part):
            c = pred(key_scr[j], j * tk + row_k).astype(jnp.int32)
            return part + jnp.sum(c, axis=0, keepdims=True)
        return lax.fori_loop(0, nkv, blk, jnp.zeros((1, tq), jnp.int32))

    n_nonneg = count(lambda k, _: k >= 0)
    tau = jnp.where(n_nonneg >= top_k, 0, INT_MIN).astype(jnp.int32)
    n_ge = jnp.where(n_nonneg >= top_k, n_nonneg, nkv * tk)

    def tau_bit(it, carry):
        tau, n_ge = carry
        cand = tau | (jnp.int32(1) << (30 - it))
        c = count(lambda k, _: k >= cand)
        return jnp.where(c >= top_k, cand, tau), jnp.where(c >= top_k, c, n_ge)

    tau, n_ge = lax.fori_loop(0, 31, tau_bit, (tau, n_ge))

    def tie_search():
        need = top_k - count(lambda k, _: k > tau)

        def tie_bit(it, mi):
            step = jnp.int32(1) << (idx_bits - 1 - it)
            c = count(lambda k, idx: (k == tau) & (idx <= mi + step - 1))
            return jnp.where(c < need, mi + step, mi)
        return lax.fori_loop(0, idx_bits, tie_bit, jnp.zeros((1, tq), jnp.int32))

    partial_ties = (n_ge > top_k) & (tau > KEY_NEG_INF)
    mi = lax.cond(jnp.max(partial_ties.astype(jnp.int32)) > 0, tie_search,
                  lambda: jnp.full((1, tq), seq_len, jnp.int32))

    def bias_block(j, _):
        k = key_scr[j]
        idx = j * tk + row_k
        sel = (k > tau) | ((k == tau) & (idx <= mi))
        adm = (idx // CHUNK) <= qchunk
        bias_scr[j] = jnp.where(sel & adm, 0.0, NEG)
        return 0

    lax.fori_loop(0, nkv, bias_block, 0)

    for h0 in range(0, DSA_HEADS, DSA_HEADS_PER_LOOP):
        heads = tuple(range(h0, h0 + DSA_HEADS_PER_LOOP))
        qs = [qd_ref[:, h * DSA_HEAD_DIM:(h + 1) * DSA_HEAD_DIM] for h in heads]

        def att(j, carries, heads=heads, qs=qs):
            off = pl.multiple_of(j * tk, tk)
            sls = [slice(h * DSA_HEAD_DIM, (h + 1) * DSA_HEAD_DIM) for h in heads]
            kbs = [kd_ref[pl.ds(off, tk), sl] for sl in sls]
            vtbs = [vt_ref[j, sl, :] for sl in sls]
            return _flash_t_steps(carries, qs, kbs, vtbs, bias_scr[j], s_scr)

        carries = lax.fori_loop(0, nkv, att, tuple(_flash_t_init(tq, DSA_HEAD_DIM) for _ in heads))
        for h, (_, l, acc) in zip(heads, carries):
            o_ref[:, h * DSA_HEAD_DIM:(h + 1) * DSA_HEAD_DIM] = (acc / l).T.astype(o_ref.dtype)


def _dsa_attn(qd, qi, wit, kd, vt, kia, kib, B, S, tq, tk, top_k):
    T = B * S
    nq = S // tq
    hd = DSA_HEADS * DSA_HEAD_DIM
    hi = IDX_HEADS * IDX_HEAD_DIM
    idx_bits = max(1, (S - 1).bit_length())
    assert vt.shape == (T // tk, hd, tk)
    qrow = lambda w: pl.BlockSpec((tq, w), lambda b, i: (b * nq + i, 0))
    seq = lambda w: pl.BlockSpec((S, w), lambda b, i: (b, 0))
    return pl.pallas_call(
        functools.partial(_dsa_attn_body, tq=tq, tk=tk, top_k=top_k, idx_bits=idx_bits, seq_len=S),
        grid=(B, nq),
        in_specs=[qrow(hd), qrow(hi), pl.BlockSpec((IDX_HEADS, tq), lambda b, i: (0, b * nq + i)), seq(hd),
                  pl.BlockSpec((S // tk, hd, tk), lambda b, i: (b, 0, 0)), seq(LANES), seq(LANES)],
        out_specs=qrow(hd),
        out_shape=jax.ShapeDtypeStruct((T, hd), jnp.bfloat16),
        scratch_shapes=[pltpu.VMEM((S // tk, tk, tq), jnp.int32), pltpu.VMEM((S // tk, tk, tq), jnp.float32),
                        pltpu.VMEM((DSA_HEADS_PER_LOOP, tk, tq), jnp.float32)],
        compiler_params=_cparams(("arbitrary", "arbitrary")),
        name="dsa_attn",
    )(qd, qi, wit, kd, vt, kia, kib)


def _merge_body(oa_ref, ob_ref, gc_ref, x_ref, woa_ref, wob_ref, wout_ref, bg_ref, gf_ref, wr_ref, br_ref,
                x1_ref, hn_ref, lg_ref):
    ya = jnp.dot(oa_ref[...], woa_ref[...], preferred_element_type=jnp.float32)
    yb = jnp.dot(ob_ref[...], wob_ref[...], preferred_element_type=jnp.float32)
    g = jax.nn.sigmoid(gc_ref[...].astype(jnp.float32) + bg_ref[...])
    merged = g[:, :D_MODEL] * ya + g[:, D_MODEL:] * yb
    x1 = x_ref[...] + jnp.dot(merged.astype(jnp.bfloat16), wout_ref[...], preferred_element_type=jnp.float32)
    x1_ref[...] = x1
    ms = jnp.mean(x1 * x1, axis=-1, keepdims=True)
    hn = x1 * lax.rsqrt(ms + NORM_EPS) * gf_ref[...]
    hn_ref[...] = hn.reshape(hn_ref.shape)
    lg_ref[...] = lax.dot_general(wr_ref[...], hn.astype(jnp.bfloat16), _NT,
                                  preferred_element_type=jnp.float32) + br_ref[...]


def _merge(oa, ob, proj, x2, woa, wob, wout, bg, gf, wr_t, br, tm):
    T = x2.shape[0]
    row = lambda w: pl.BlockSpec((tm, w), lambda i: (i, 0))
    full = lambda shape: pl.BlockSpec(shape, lambda i: (0, 0))
    return pl.pallas_call(
        _merge_body,
        grid=(T // tm,),
        in_specs=[row(D_MODEL), row(D_MODEL), row(2 * D_MODEL), row(D_MODEL),
                  full((D_MODEL, D_MODEL)), full((D_MODEL, D_MODEL)), full((D_MODEL, D_MODEL)),
                  full((1, 2 * D_MODEL)), full((1, D_MODEL)), full((N_EXPERTS, D_MODEL)), full((N_EXPERTS, 1))],
        out_specs=[row(D_MODEL), pl.BlockSpec((tm,) + ROW_TILE, lambda i: (i, 0, 0)),
                   pl.BlockSpec((N_EXPERTS, tm), lambda i: (0, i))],
        out_shape=[
            jax.ShapeDtypeStruct((T, D_MODEL), jnp.float32),
            jax.ShapeDtypeStruct((T,) + ROW_TILE, jnp.float32),
            jax.ShapeDtypeStruct((N_EXPERTS, T), jnp.float32),
        ],
        compiler_params=_cparams(("arbitrary",)),
        name="merge",
    )(oa, ob, proj, x2, woa, wob, wout, bg, gf, wr_t, br)


def _route_body(lg_ref, eidx_ref, gate_ref, rank_ref, cnt_ref, carry_scr, *, tr):
    @pl.when(pl.program_id(0) == 0)
    def _():
        carry_scr[...] = jnp.zeros_like(carry_scr)

    vals = lg_ref[...]
    rows = lax.broadcasted_iota(jnp.int32, vals.shape, 0)
    tops, hots = [], []
    for k in range(MOE_TOP_K):
        m = jnp.max(vals, axis=0, keepdims=True)
        idx = jnp.min(jnp.where(vals == m, rows, N_EXPERTS), axis=0, keepdims=True)
        hot = rows == idx
        vals = jnp.where(hot, -jnp.inf, vals)
        tops.append(m)
        hots.append(hot)
        eidx_ref[k:k + 1, :] = idx
    es = [jnp.exp(m - tops[0]) for m in tops]
    den = es[0] + es[1] + es[2] + es[3]
    for k in range(MOE_TOP_K):
        gate_ref[k:k + 1, :] = es[k] / den
    sel = jnp.zeros(vals.shape, jnp.float32)
    for hot in hots:
        sel = sel + hot.astype(jnp.float32)
    before = (lax.broadcasted_iota(jnp.int32, (tr, tr), 0) < lax.broadcasted_iota(jnp.int32, (tr, tr), 1))
    prefix = jnp.dot(sel.astype(jnp.bfloat16), before.astype(jnp.bfloat16), preferred_element_type=jnp.float32)
    prefix = prefix + carry_scr[:, 0:1]
    for k in range(MOE_TOP_K):
        rank_ref[k:k + 1, :] = jnp.sum(jnp.where(hots[k], prefix, 0.0), axis=0, keepdims=True).astype(jnp.int32)
    carry_scr[...] = carry_scr[...] + jnp.sum(sel, axis=1, keepdims=True)
    cnt_ref[...] = carry_scr[...]


def _route(logits_t, tr):
    T = logits_t.shape[1]
    tok = pl.BlockSpec((MOE_TOP_K, tr), lambda i: (0, i))
    return pl.pallas_call(
        functools.partial(_route_body, tr=tr),
        grid=(T // tr,),
        in_specs=[pl.BlockSpec((N_EXPERTS, tr), lambda i: (0, i))],
        out_specs=[tok, tok, tok, pl.BlockSpec((N_EXPERTS, LANES), lambda i: (0, 0))],
        out_shape=[
            jax.ShapeDtypeStruct((MOE_TOP_K, T), jnp.int32),
            jax.ShapeDtypeStruct((MOE_TOP_K, T), jnp.float32),
            jax.ShapeDtypeStruct((MOE_TOP_K, T), jnp.int32),
            jax.ShapeDtypeStruct((N_EXPERTS, LANES), jnp.float32),
        ],
        scratch_shapes=[pltpu.VMEM((N_EXPERTS, LANES), jnp.float32)],
        compiler_params=_cparams(("arbitrary",)),
        name="route",
    )(logits_t)


def _dest_body(poff_ref, eidx_ref, rank_ref, dest_ref):
    e = eidx_ref[...]
    base = jnp.zeros(e.shape, jnp.int32)
    for x in range(N_EXPERTS):
        base = jnp.where(e == x, poff_ref[x], base)
    dest_ref[...] = base + rank_ref[...]


def _dest(pad_off, eidx, rank, tr):
    T = eidx.shape[1]
    tok = pl.BlockSpec((MOE_TOP_K, tr), lambda i, p: (0, i))
    return pl.pallas_call(
        _dest_body,
        grid_spec=pltpu.PrefetchScalarGridSpec(
            num_scalar_prefetch=1, grid=(T // tr,), in_specs=[tok, tok], out_specs=tok),
        out_shape=jax.ShapeDtypeStruct((MOE_TOP_K, T), jnp.int32),
        compiler_params=_cparams(("arbitrary",)),
        name="dest",
    )(pad_off, eidx, rank)


def _dispatch_body(zstart_ref, zvalid_ref, dest_ref, dest_prev_ref, hn_ref, xs_ref, z_scr, zsem, hbuf, lsems, rsems,
                   *, tm):
    @pl.when(pl.program_id(0) == 0)
    def _():
        z_scr[...] = jnp.zeros_like(z_scr)

        def zero_copy(n):
            return pltpu.make_async_copy(z_scr, xs_ref.at[pl.ds(zstart_ref[n], MOE_BLOCK)], zsem)

        def zissue(n, _):
            @pl.when(zvalid_ref[n] > 0)
            def _():
                zero_copy(n).start()
            return 0

        def zdrain(n, _):
            @pl.when(zvalid_ref[n] > 0)
            def _():
                zero_copy(n).wait()
            return 0

        lax.fori_loop(0, 2 * N_EXPERTS, zissue, 0)
        lax.fori_loop(0, 2 * N_EXPERTS, zdrain, 0)

    i = pl.program_id(0)
    n = pl.num_programs(0)

    def tile_load(step):
        s = step % DISPATCH_SLOTS
        return pltpu.make_async_copy(hn_ref.at[pl.ds(step * tm, tm)], hbuf.at[s], lsems.at[s])

    def row_copy(dref, s, k, t):
        return pltpu.make_async_copy(hbuf.at[s, t], xs_ref.at[dref[k, t]], rsems.at[s])

    def drain_rows(dref, s):
        def drain(t, _):
            for k in range(MOE_TOP_K):
                row_copy(dref, s, k, t).wait()
            return 0
        lax.fori_loop(0, tm, drain, 0, unroll=DRAIN_UNROLL)

    @pl.when(i == 0)
    def _():
        tile_load(0).start()

        @pl.when(n > 1)
        def _():
            tile_load(1).start()

    slot = i % DISPATCH_SLOTS
    tile_load(i).wait()

    def issue(t, _):
        for k in range(MOE_TOP_K):
            row_copy(dest_ref, slot, k, t).start(priority=k % DMA_PRIORITIES)
        return 0

    lax.fori_loop(0, tm, issue, 0, unroll=ISSUE_UNROLL)

    @pl.when(i > 0)
    def _():
        drain_rows(dest_prev_ref, (i - 1) % DISPATCH_SLOTS)

    @pl.when(i + 2 < n)
    def _():
        tile_load(i + 2).start()

    @pl.when(i == n - 1)
    def _():
        drain_rows(dest_ref, slot)


def _dispatch(zero_start, zero_valid, dest, hn, P, tm):
    T = hn.shape[0]
    return pl.pallas_call(
        functools.partial(_dispatch_body, tm=tm),
        grid_spec=pltpu.PrefetchScalarGridSpec(
            num_scalar_prefetch=2, grid=(T // tm,),
            in_specs=[
                pl.BlockSpec((MOE_TOP_K, tm), lambda i, zs, zv: (0, i), memory_space=pltpu.SMEM),
                pl.BlockSpec((MOE_TOP_K, tm), lambda i, zs, zv: (0, jnp.maximum(i - 1, 0)), memory_space=pltpu.SMEM),
                pl.BlockSpec(memory_space=pl.ANY),
            ],
            out_specs=pl.BlockSpec(memory_space=pl.ANY),
            scratch_shapes=[pltpu.VMEM((MOE_BLOCK,) + ROW_TILE, hn.dtype), pltpu.SemaphoreType.DMA(()),
                            pltpu.VMEM((DISPATCH_SLOTS, tm) + ROW_TILE, hn.dtype),
                            pltpu.SemaphoreType.DMA((DISPATCH_SLOTS,)), pltpu.SemaphoreType.DMA((DISPATCH_SLOTS,))]),
        out_shape=jax.ShapeDtypeStruct((P,) + ROW_TILE, hn.dtype),
        compiler_params=_cparams(("arbitrary",)),
        name="dispatch",
    )(zero_start, zero_valid, dest, dest, hn)


def _experts_body(bexp_ref, first_ref, nused_ref, xs_ref, wgu_ref, bgu_ref, wd_ref, bd_ref, y_ref, wgu_s, wd_s):
    j = pl.program_id(0)

    @pl.when(first_ref[j] > 0)
    def _():
        wgu_s[...] = wgu_ref[0].astype(jnp.bfloat16)
        wd_s[...] = wd_ref[0].astype(jnp.bfloat16)

    @pl.when(j < nused_ref[0])
    def _():
        xb = xs_ref[...].reshape(MOE_BLOCK, D_MODEL).astype(jnp.bfloat16)
        h = jnp.dot(xb, wgu_s[...], preferred_element_type=jnp.float32) + bgu_ref[0]
        gate = jnp.minimum(h[:, :D_EXPERT], SWIGLU_LIMIT)
        lin = jnp.clip(h[:, D_EXPERT:], -SWIGLU_LIMIT, SWIGLU_LIMIT)
        glu = gate * jax.nn.sigmoid(SWIGLU_ALPHA * gate)
        act = ((lin + 1.0) * glu).astype(jnp.bfloat16)
        y = jnp.dot(act, wd_s[...], preferred_element_type=jnp.float32) + bd_ref[0]
        y_ref[...] = y.reshape(y_ref.shape)

    @pl.when(j >= nused_ref[0])
    def _():
        y_ref[...] = jnp.zeros_like(y_ref)


def _experts(blk_expert, blk_first, n_used, xs, wgu, bgu, wd, bd):
    P = xs.shape[0]
    nb = P // MOE_BLOCK
    rows = pl.BlockSpec((MOE_BLOCK,) + ROW_TILE, lambda j, be, bf, nu: (j, 0, 0))
    return pl.pallas_call(
        _experts_body,
        grid_spec=pltpu.PrefetchScalarGridSpec(
            num_scalar_prefetch=3, grid=(nb,),
            in_specs=[
                rows,
                pl.BlockSpec((1, D_MODEL, 2 * D_EXPERT), lambda j, be, bf, nu: (be[j], 0, 0)),
                pl.BlockSpec((1, 1, 2 * D_EXPERT), lambda j, be, bf, nu: (be[j], 0, 0)),
                pl.BlockSpec((1, D_EXPERT, D_MODEL), lambda j, be, bf, nu: (be[j], 0, 0)),
                pl.BlockSpec((1, 1, D_MODEL), lambda j, be, bf, nu: (be[j], 0, 0)),
            ],
            out_specs=rows,
            scratch_shapes=[pltpu.VMEM((D_MODEL, 2 * D_EXPERT), jnp.bfloat16),
                            pltpu.VMEM((D_EXPERT, D_MODEL), jnp.bfloat16)]),
        out_shape=jax.ShapeDtypeStruct((P,) + ROW_TILE, jnp.float32),
        compiler_params=_cparams(("arbitrary",)),
        name="experts",
    )(blk_expert, blk_first, n_used, xs, wgu, bgu, wd, bd)


def _combine_body(dest_ref, dest_next_ref, y_ref, x1_ref, gate_ref, gn_ref, o_ref, ybuf, sems, *, tm):
    i = pl.program_id(0)
    slot = i % 2

    def row_copy(dref, s, k, t):
        return pltpu.make_async_copy(y_ref.at[dref[k, t]], ybuf.at[s, k, t], sems.at[s])

    def issue_rows(dref, s):
        def issue(t, _):
            for k in range(MOE_TOP_K):
                row_copy(dref, s, k, t).start(priority=k % DMA_PRIORITIES)
            return 0
        lax.fori_loop(0, tm, issue, 0, unroll=ISSUE_UNROLL)

    @pl.when(i == 0)
    def _():
        issue_rows(dest_ref, 0)

    @pl.when(i + 1 < pl.num_programs(0))
    def _():
        issue_rows(dest_next_ref, 1 - slot)

    def drain(t, _):
        for k in range(MOE_TOP_K):
            row_copy(dest_ref, slot, k, t).wait()
        return 0

    lax.fori_loop(0, tm, drain, 0, unroll=DRAIN_UNROLL)

    g = gate_ref[...]
    acc = x1_ref[...]
    for k in range(MOE_TOP_K):
        acc = acc + g[:, k:k + 1] * ybuf[slot, k].reshape(tm, D_MODEL)
    ms = jnp.mean(acc * acc, axis=-1, keepdims=True)
    o_ref[...] = acc * lax.rsqrt(ms + NORM_EPS) * gn_ref[...]


def _combine(dest, y, x1, gate_t, gn, tm):
    T = x1.shape[0]
    n = T // tm
    return pl.pallas_call(
        functools.partial(_combine_body, tm=tm),
        grid=(n,),
        in_specs=[
            pl.BlockSpec((MOE_TOP_K, tm), lambda i: (0, i), memory_space=pltpu.SMEM),
            pl.BlockSpec((MOE_TOP_K, tm), lambda i: (0, jnp.minimum(i + 1, n - 1)), memory_space=pltpu.SMEM),
            pl.BlockSpec(memory_space=pl.ANY),
            pl.BlockSpec((tm, D_MODEL), lambda i: (i, 0)),
            pl.BlockSpec((tm, MOE_TOP_K), lambda i: (i, 0)),
            pl.BlockSpec((1, D_MODEL), lambda i: (0, 0)),
        ],
        out_specs=pl.BlockSpec((tm, D_MODEL), lambda i: (i, 0)),
        out_shape=jax.ShapeDtypeStruct((T, D_MODEL), jnp.float32),
        scratch_shapes=[pltpu.VMEM((2, MOE_TOP_K, tm) + ROW_TILE, jnp.float32), pltpu.SemaphoreType.DMA((2,))],
        compiler_params=_cparams(("arbitrary",)),
        name="combine",
    )(dest, dest, y, x1, gate_t, gn)


def _rope_tables(S):
    pos = jnp.arange(S, dtype=jnp.float32)[:, None]

    def cs(half):
        inv = ROPE_THETA ** (-jnp.arange(half, dtype=jnp.float32) / half)
        ang = pos * inv[None, :]
        return jnp.cos(ang), jnp.sin(ang)

    z = lambda n: jnp.zeros((S, n), jnp.float32)
    o = lambda n: jnp.ones((S, n), jnp.float32)
    cat = lambda *a: jnp.concatenate(a, axis=1)
    c, s = cs(MLA_ROPE // 2)
    mla = (cat(c, c, z(64)), cat(z(32), s, z(64)), cat(-s, z(96)))
    c, s = cs(DSA_HEAD_DIM // 2)
    dsa = (cat(c, c), cat(-s, s))
    c, s = cs(IDX_ROPE // 2)
    c1, sa1, sb1 = cat(c, c, o(32)), cat(z(16), s, z(32)), cat(-s, z(48))
    idx_q = (cat(c1, c1), cat(sa1, sa1), cat(sb1, sb1))
    idx_k = (cat(c1, o(64)), cat(sa1, z(64)), cat(sb1, z(64)))
    return mla, dsa + idx_q + idx_k


def _pack_w_in(w):
    sizes = (MLA_Q_LORA, MLA_KV_LORA, MLA_ROPE, 3 * DSA_HEADS * DSA_HEAD_DIM, IDX_HEADS * IDX_HEAD_DIM,
             IDX_HEAD_DIM, IDX_HEADS, 2 * D_MODEL)
    pts, acc = [], 0
    for s in sizes[:-1]:
        acc += s
        pts.append(acc)
    w = w.astype(jnp.bfloat16)
    c_q, c_kv, k_r, qkv, q_idx, k_idx, w_idx, gates = jnp.split(w, pts, axis=1)
    z = lambda n: jnp.zeros((w.shape[0], n), w.dtype)
    packed = jnp.concatenate([gates, qkv, q_idx, k_idx, w_idx, z(LANES - IDX_HEAD_DIM - IDX_HEADS),
                              c_q, c_kv, k_r, z(LANES - MLA_ROPE)], axis=1)
    assert packed.shape[1] == D_IN_PACKED
    return packed


def _pack_w_uq(w):
    w = w.reshape(MLA_Q_LORA, MLA_HEADS, MLA_QK)
    w = jnp.pad(w, ((0, 0), (0, 0), (0, MLA_QK_PAD - MLA_QK)))
    return w.reshape(MLA_Q_LORA, MLA_HEADS * MLA_QK_PAD).astype(jnp.bfloat16)


def _tile(n, pref):
    t = min(n, pref)
    assert n % t == 0
    return t


def kernel(x, norm_attn_g, w_in, q_lora_g, kv_lora_g, w_uq, w_uk, w_uv, w_o_mla, w_o_dsa, b_gate, w_out,
           norm_ffn_g, w_router, b_router, w_gate_up, b_gate_up, w_down, b_down, norm_final_g):
    B, S, D = x.shape
    assert D == D_MODEL and S % CHUNK == 0 and norm_attn_g.shape[0] == 1
    T = B * S
    top_k = min(DSA_TOPK_MAX, S // 4)
    bf = jnp.bfloat16
    x2 = x.reshape(T, D)
    mla_tabs, dsa_tabs = _rope_tables(S)

    proj = _in_proj(x2, norm_attn_g, _pack_w_in(w_in[0]), _tile(T, 1024), D_IN_PACKED // 3)
    tk = _tile(S, 512)
    tq = _tile(S, 512)
    q, k, vt = _mla_prep(proj, q_lora_g, kv_lora_g, _pack_w_uq(w_uq[0]), w_uk[0].astype(bf), w_uv[0].T.astype(bf),
                         mla_tabs, S, tk)
    o_mla = _mla_attn(q, k, vt, B, S, tq, tk)
    qd, kd, vtd, qi, kia, kib, wit = _dsa_prep(proj, dsa_tabs, S, tk)
    o_dsa = _dsa_attn(qd, qi, wit, kd, vtd, kia, kib, B, S, tq, tk, top_k)
    x1, hn, logits_t = _merge(o_mla, o_dsa, proj, x2, w_o_mla[0].astype(bf), w_o_dsa[0].astype(bf),
                              w_out[0].astype(bf), b_gate, norm_ffn_g, w_router[0].T.astype(bf),
                              b_router[0][:, None], _tile(T, 512))

    tr = _tile(T, 512)
    eidx, gate, rank, cnt = _route(logits_t, tr)
    counts = cnt[:, 0].astype(jnp.int32)
    padded = ((counts + MOE_BLOCK - 1) // MOE_BLOCK) * MOE_BLOCK
    pad_end = jnp.cumsum(padded)
    pad_off = pad_end - padded
    P = T * MOE_TOP_K + N_EXPERTS * MOE_BLOCK
    nb = P // MOE_BLOCK
    blk_start = jnp.arange(nb, dtype=jnp.int32) * MOE_BLOCK
    blk_expert = jnp.minimum(jnp.sum(pad_end[None, :] <= blk_start[:, None], axis=1), N_EXPERTS - 1).astype(jnp.int32)
    blk_first = jnp.concatenate([jnp.ones((1,), jnp.int32), (blk_expert[1:] != blk_expert[:-1]).astype(jnp.int32)])
    n_used = (pad_end[-1:] // MOE_BLOCK).astype(jnp.int32)

    dest = _dest(pad_off.astype(jnp.int32), eidx, rank, tr)
    trailing = n_used[0] + jnp.arange(N_EXPERTS, dtype=jnp.int32)
    zero_start = jnp.concatenate([pad_end - MOE_BLOCK, jnp.minimum(trailing, nb - 1) * MOE_BLOCK]).astype(jnp.int32)
    zero_valid = jnp.concatenate([padded > 0, trailing < nb]).astype(jnp.int32)
    xs = _dispatch(zero_start, zero_valid, dest, hn, P, _tile(T, 128))
    y = _experts(blk_expert, blk_first, n_used, xs, w_gate_up[0], b_gate_up[0][:, None, :],
                 w_down[0], b_down[0][:, None, :])
    out = _combine(dest, y, x1, gate.T, norm_final_g[None, :], _tile(T, 128))
    return out.reshape(B, S, D)
```

```python
import functools

import jax
import jax.numpy as jnp
from jax import lax
from jax.experimental import pallas as pl
from jax.experimental.pallas import tpu as pltpu

D_MODEL = 1024
CHUNK = 64
ROPE_THETA = 10000.0
NORM_EPS = 1e-6

MLA_HEADS = 8
MLA_Q_LORA = 384
MLA_KV_LORA = 256
MLA_NOPE = 128
MLA_ROPE = 64
MLA_V = 128
MLA_QK = MLA_NOPE + MLA_ROPE
MLA_QK_PAD = 256

DSA_HEADS = 8
DSA_HEAD_DIM = 128
IDX_HEADS = 8
IDX_HEAD_DIM = 64
IDX_ROPE = 32
DSA_TOPK_MAX = 256

N_EXPERTS = 32
MOE_TOP_K = 4
D_EXPERT = 1024
SWIGLU_LIMIT = 7.0
SWIGLU_ALPHA = 1.702

LANES = 128
HEADS_PER_LOOP = 8
DSA_HEADS_PER_LOOP = 4
MOE_BLOCK = 512
DISPATCH_SLOTS = 3
DMA_PRIORITIES = 2
ISSUE_UNROLL = 4
DRAIN_UNROLL = 8
ROW_TILE = (D_MODEL // LANES, LANES)
ROW_DTYPE = jnp.bfloat16
VMEM_LIMIT = 56 * 1024 * 1024

OFF_GATES = 0
OFF_Q = 2048
OFF_K = 3072
OFF_V = 4096
OFF_QIDX = 5120
OFF_KIDX = 5632
OFF_CQ = 5760
OFF_CKV = 6144
OFF_KR = 6400
D_IN_PACKED = 6528

LOG2E = 1.4426950408889634
NEG = -1e30
INT_MIN = -(2 ** 31)
KEY_NEG_INF = -2139095041

_NT = (((1,), (1,)), ((), ()))


def _cparams(sem):
    return pltpu.CompilerParams(dimension_semantics=sem, vmem_limit_bytes=VMEM_LIMIT)


def _rope3(x, c, sa, sb, shift):
    n = x.shape[-1]
    return x * c + pltpu.roll(x, shift, 1) * sa + pltpu.roll(x, n - shift, 1) * sb


def _inproj_body(x_ref, g_ref, w_ref, o_ref, h_scr):
    @pl.when(pl.program_id(1) == 0)
    def _():
        x = x_ref[...]
        ms = jnp.mean(x * x, axis=-1, keepdims=True)
        h_scr[...] = (x * lax.rsqrt(ms + NORM_EPS) * g_ref[...]).astype(jnp.bfloat16)

    o_ref[...] = jnp.dot(h_scr[...], w_ref[...], preferred_element_type=jnp.float32).astype(o_ref.dtype)


def _in_proj(x2, g, w_packed, tm, tn):
    T = x2.shape[0]
    return pl.pallas_call(
        _inproj_body,
        grid=(T // tm, D_IN_PACKED // tn),
        in_specs=[
            pl.BlockSpec((tm, D_MODEL), lambda i, j: (i, 0)),
            pl.BlockSpec((1, D_MODEL), lambda i, j: (0, 0)),
            pl.BlockSpec((D_MODEL, tn), lambda i, j: (0, j)),
        ],
        out_specs=pl.BlockSpec((tm, tn), lambda i, j: (i, j)),
        out_shape=jax.ShapeDtypeStruct((T, D_IN_PACKED), jnp.bfloat16),
        scratch_shapes=[pltpu.VMEM((tm, D_MODEL), jnp.bfloat16)],
        compiler_params=_cparams(("arbitrary", "arbitrary")),
        name="in_proj",
    )(x2, g, w_packed)


def _mla_prep_body(cq_ref, ckv_ref, kr_ref, gq_ref, gkv_ref, wq_ref, wk_ref, wv_ref,
                   c_ref, sa_ref, sb_ref, q_ref, k_ref, v_ref):
    def norm(ref, g_ref):
        x = ref[...].astype(jnp.float32)
        ms = jnp.mean(x * x, axis=-1, keepdims=True)
        return (x * lax.rsqrt(ms + NORM_EPS) * g_ref[...]).astype(jnp.bfloat16)

    cq = norm(cq_ref, gq_ref)
    ckv = norm(ckv_ref, gkv_ref)
    c, sa, sb = c_ref[...], sa_ref[...], sb_ref[...]
    scale = MLA_QK ** -0.5 * LOG2E
    q_all = jnp.dot(cq, wq_ref[...], preferred_element_type=jnp.float32)
    kn = jnp.dot(ckv, wk_ref[...], preferred_element_type=jnp.float32)
    v_ref[0] = lax.dot_general(wv_ref[...], ckv, _NT, preferred_element_type=jnp.float32).astype(v_ref.dtype)
    kr = _rope3(kr_ref[...].astype(jnp.float32), c, sa, sb, MLA_ROPE // 2).astype(k_ref.dtype)
    for h in range(MLA_HEADS):
        o = h * MLA_QK_PAD
        q_ref[:, o:o + LANES] = (q_all[:, o:o + LANES] * scale).astype(q_ref.dtype)
        qr = _rope3(q_all[:, o + LANES:o + 2 * LANES], c, sa, sb, MLA_ROPE // 2)
        q_ref[:, o + LANES:o + 2 * LANES] = (qr * scale).astype(q_ref.dtype)
        k_ref[:, o:o + LANES] = kn[:, h * LANES:(h + 1) * LANES].astype(k_ref.dtype)
        k_ref[:, o + LANES:o + 2 * LANES] = kr


def _mla_prep(proj, gq, gkv, wq, wk, wv, tabs, S, tm):
    T = proj.shape[0]
    ns = S // tm
    full = lambda shape: pl.BlockSpec(shape, lambda i: (0, 0))
    tab = pl.BlockSpec((tm, LANES), lambda i: (i % ns, 0))
    hq = MLA_HEADS * MLA_QK_PAD
    return pl.pallas_call(
        _mla_prep_body,
        grid=(T // tm,),
        in_specs=[
            pl.BlockSpec((tm, MLA_Q_LORA), lambda i: (i, OFF_CQ // MLA_Q_LORA)),
            pl.BlockSpec((tm, MLA_KV_LORA), lambda i: (i, OFF_CKV // MLA_KV_LORA)),
            pl.BlockSpec((tm, LANES), lambda i: (i, OFF_KR // LANES)),
            full((1, MLA_Q_LORA)), full((1, MLA_KV_LORA)),
            full((MLA_Q_LORA, hq)), full((MLA_KV_LORA, MLA_HEADS * MLA_NOPE)),
            full((MLA_HEADS * MLA_V, MLA_KV_LORA)),
            tab, tab, tab,
        ],
        out_specs=[
            pl.BlockSpec((tm, hq), lambda i: (i, 0)),
            pl.BlockSpec((tm, hq), lambda i: (i, 0)),
            pl.BlockSpec((1, MLA_HEADS * MLA_V, tm), lambda i: (i, 0, 0)),
        ],
        out_shape=[
            jax.ShapeDtypeStruct((T, hq), jnp.bfloat16),
            jax.ShapeDtypeStruct((T, hq), jnp.bfloat16),
            jax.ShapeDtypeStruct((T // tm, MLA_HEADS * MLA_V, tm), jnp.bfloat16),
        ],
        compiler_params=_cparams(("arbitrary",)),
        name="mla_prep",
    )(proj, proj, proj, gq, gkv, wq, wk, wv, *tabs)


def _flash_t_init(tq, dv):
    return (jnp.full((1, tq), NEG, jnp.float32), jnp.zeros((1, tq), jnp.float32), jnp.zeros((dv, tq), jnp.float32))


def _flash_t_steps(carries, qs, kbs, vtbs, bias, s_scr):
    for n, (qh, kb) in enumerate(zip(qs, kbs)):
        s_scr[n] = lax.dot_general(kb, qh, _NT, preferred_element_type=jnp.float32)
    out = []
    for n, ((m, l, acc), vtb) in enumerate(zip(carries, vtbs)):
        s = s_scr[n]
        if bias is not None:
            s = s + bias
        m_new = jnp.maximum(m, jnp.max(s, axis=0, keepdims=True))
        alpha = jnp.exp2(m - m_new)
        p = jnp.exp2(s - m_new)
        l = alpha * l + jnp.sum(p, axis=0, keepdims=True)
        acc = alpha * acc + jnp.dot(vtb, p.astype(vtb.dtype), preferred_element_type=jnp.float32)
        out.append((m_new, l, acc))
    return tuple(out)


def _mla_attn_body(q_ref, k_ref, vt_ref, o_ref, s_scr, *, tq, tk):
    i = pl.program_id(1)
    q0 = i * tq
    nkv = (q0 + tq + tk - 1) // tk
    last = nkv - 1
    kc = (last * tk + lax.broadcasted_iota(jnp.int32, (tk, tq), 0)) // CHUNK
    qc = (q0 + lax.broadcasted_iota(jnp.int32, (tk, tq), 1)) // CHUNK
    last_bias = jnp.where(kc <= qc, 0.0, NEG)

    for h0 in range(0, MLA_HEADS, HEADS_PER_LOOP):
        heads = tuple(range(h0, h0 + HEADS_PER_LOOP))
        qs = [q_ref[:, h * MLA_QK_PAD:(h + 1) * MLA_QK_PAD] for h in heads]

        def step(j, carries, bias, heads=heads, qs=qs):
            off = pl.multiple_of(j * tk, tk)
            kbs = [k_ref[pl.ds(off, tk), h * MLA_QK_PAD:(h + 1) * MLA_QK_PAD] for h in heads]
            vtbs = [vt_ref[j, h * MLA_V:(h + 1) * MLA_V, :] for h in heads]
            return _flash_t_steps(carries, qs, kbs, vtbs, bias, s_scr)

        init = tuple(_flash_t_init(tq, MLA_V) for _ in heads)
        carries = lax.fori_loop(0, last, lambda j, c: step(j, c, None), init)
        carries = step(last, carries, last_bias)
        for h, (_, l, acc) in zip(heads, carries):
            o_ref[:, h * MLA_V:(h + 1) * MLA_V] = (acc / l).T.astype(o_ref.dtype)


def _mla_attn(q, k, vt, B, S, tq, tk):
    T = B * S
    nq = S // tq
    hq = MLA_HEADS * MLA_QK_PAD
    hv = MLA_HEADS * MLA_V
    assert vt.shape == (T // tk, hv, tk)
    return pl.pallas_call(
        functools.partial(_mla_attn_body, tq=tq, tk=tk),
        grid=(B, nq),
        in_specs=[
            pl.BlockSpec((tq, hq), lambda b, i: (b * nq + i, 0)),
            pl.BlockSpec((S, hq), lambda b, i: (b, 0)),
            pl.BlockSpec((S // tk, hv, tk), lambda b, i: (b, 0, 0)),
        ],
        out_specs=pl.BlockSpec((tq, hv), lambda b, i: (b * nq + i, 0)),
        out_shape=jax.ShapeDtypeStruct((T, hv), jnp.bfloat16),
        scratch_shapes=[pltpu.VMEM((HEADS_PER_LOOP, tk, tq), jnp.float32)],
        compiler_params=_cparams(("arbitrary", "arbitrary")),
        name="mla_attn",
    )(q, k, vt)


def _dsa_prep_body(q_ref, k_ref, v_ref, qi_ref, kx_ref, cd_ref, sd_ref, ci_ref, sai_ref, sbi_ref,
                   ck_ref, sak_ref, sbk_ref, qo_ref, ko_ref, vt_ref, qio_ref, kia_ref, kib_ref, wit_ref):
    cd, sd = cd_ref[...], sd_ref[...]
    for h in range(DSA_HEADS):
        sl = slice(h * LANES, (h + 1) * LANES)
        xq = q_ref[:, sl].astype(jnp.float32)
        xk = k_ref[:, sl].astype(jnp.float32)
        rq = xq * cd + pltpu.roll(xq, DSA_HEAD_DIM // 2, 1) * sd
        rk = xk * cd + pltpu.roll(xk, DSA_HEAD_DIM // 2, 1) * sd
        qo_ref[:, sl] = (rq * (DSA_HEAD_DIM ** -0.5 * LOG2E)).astype(qo_ref.dtype)
        ko_ref[:, sl] = rk.astype(ko_ref.dtype)
        vt_ref[0, sl, :] = v_ref[:, sl].astype(jnp.float32).T.astype(vt_ref.dtype)
    ci, sai, sbi = ci_ref[...], sai_ref[...], sbi_ref[...]
    for p in range(IDX_HEADS * IDX_HEAD_DIM // LANES):
        sl = slice(p * LANES, (p + 1) * LANES)
        r = _rope3(qi_ref[:, sl].astype(jnp.float32), ci, sai, sbi, IDX_ROPE // 2)
        qio_ref[:, sl] = (r * (IDX_HEAD_DIM ** -0.5)).astype(qio_ref.dtype)
    kx = kx_ref[...].astype(jnp.float32)
    kr = _rope3(kx, ck_ref[...], sak_ref[...], sbk_ref[...], IDX_ROPE // 2)
    lane = lax.broadcasted_iota(jnp.int32, kr.shape, 1)
    ka = jnp.where(lane < IDX_HEAD_DIM, kr, 0.0)
    kia_ref[...] = ka.astype(kia_ref.dtype)
    kib_ref[...] = pltpu.roll(ka, IDX_HEAD_DIM, 1).astype(kib_ref.dtype)
    wit_ref[...] = kx.T[IDX_HEAD_DIM:IDX_HEAD_DIM + IDX_HEADS, :] * (IDX_HEADS ** -0.5)


def _dsa_prep(proj, tabs, S, tm):
    T = proj.shape[0]
    ns = S // tm
    tab = pl.BlockSpec((tm, LANES), lambda i: (i % ns, 0))
    hd = DSA_HEADS * DSA_HEAD_DIM
    hi = IDX_HEADS * IDX_HEAD_DIM
    row = lambda w, off: pl.BlockSpec((tm, w), lambda i: (i, off // w))
    out = lambda w: pl.BlockSpec((tm, w), lambda i: (i, 0))
    return pl.pallas_call(
        _dsa_prep_body,
        grid=(T // tm,),
        in_specs=[row(hd, OFF_Q), row(hd, OFF_K), row(hd, OFF_V), row(hi, OFF_QIDX), row(LANES, OFF_KIDX)] + [tab] * 8,
        out_specs=[out(hd), out(hd), pl.BlockSpec((1, hd, tm), lambda i: (i, 0, 0)), out(hi), out(LANES), out(LANES),
                   pl.BlockSpec((IDX_HEADS, tm), lambda i: (0, i))],
        out_shape=[
            jax.ShapeDtypeStruct((T, hd), jnp.bfloat16),
            jax.ShapeDtypeStruct((T, hd), jnp.bfloat16),
            jax.ShapeDtypeStruct((T // tm, hd, tm), jnp.bfloat16),
            jax.ShapeDtypeStruct((T, hi), jnp.bfloat16),
            jax.ShapeDtypeStruct((T, LANES), jnp.bfloat16),
            jax.ShapeDtypeStruct((T, LANES), jnp.bfloat16),
            jax.ShapeDtypeStruct((IDX_HEADS, T), jnp.float32),
        ],
        compiler_params=_cparams(("arbitrary",)),
        name="dsa_prep",
    )(proj, proj, proj, proj, proj, *tabs)


def _dsa_attn_body(qd_ref, qi_ref, wit_ref, kd_ref, vt_ref, kia_ref, kib_ref, o_ref, key_scr, bias_scr, s_scr,
                   *, tq, tk, top_k, idx_bits, seq_len):
    i = pl.program_id(1)
    q0 = i * tq
    nkv = (q0 + tq + tk - 1) // tk
    qchunk = (q0 + lax.broadcasted_iota(jnp.int32, (1, tq), 1)) // CHUNK
    row_k = lax.broadcasted_iota(jnp.int32, (tk, tq), 0)
    wit = wit_ref[...]

    def idx_block(j, _):
        off = pl.multiple_of(j * tk, tk)
        ka = kia_ref[pl.ds(off, tk), :]
        kb = kib_ref[pl.ds(off, tk), :]
        score = jnp.zeros((tk, tq), jnp.float32)
        for p in range(IDX_HEADS // 2):
            qp = qi_ref[:, p * LANES:(p + 1) * LANES]
            ra = lax.dot_general(ka, qp, _NT, preferred_element_type=jnp.float32)
            rb = lax.dot_general(kb, qp, _NT, preferred_element_type=jnp.float32)
            score = score + wit[2 * p:2 * p + 1, :] * jnp.maximum(ra, 0.0)
            score = score + wit[2 * p + 1:2 * p + 2, :] * jnp.maximum(rb, 0.0)
        adm = ((off + row_k) // CHUNK) <= qchunk
        score = jnp.where(adm, score + 0.0, -jnp.inf)
        b = pltpu.bitcast(score, jnp.int32)
        key_scr[j] = b ^ ((b >> 31) & 0x7FFFFFFF)
        return 0

    lax.fori_loop(0, nkv, idx_block, 0)

    def count(pred):
        def blk(j, part):
            c = pred(key_scr[j], j * tk + row_k).astype(jnp.int32)
            return part + jnp.sum(c, axis=0, keepdims=True)
        return lax.fori_loop(0, nkv, blk, jnp.zeros((1, tq), jnp.int32))

    n_nonneg = count(lambda k, _: k >= 0)
    tau = jnp.where(n_nonneg >= top_k, 0, INT_MIN).astype(jnp.int32)
    n_ge = jnp.where(n_nonneg >= top_k, n_nonneg, nkv * tk)

    def tau_bit(it, carry):
        tau, n_ge = carry
        cand = tau | (jnp.int32(1) << (30 - it))
        c = count(lambda k, _: k >= cand)
        return jnp.where(c >= top_k, cand, tau), jnp.where(c >= top_k, c, n_ge)

    tau, n_ge = lax.fori_loop(0, 31, tau_bit, (tau, n_ge))

    def tie_search():
        need = top_k - count(lambda k, _: k > tau)

        def tie_bit(it, mi):
            step = jnp.int32(1) << (idx_bits - 1 - it)
            c = count(lambda k, idx: (k == tau) & (idx <= mi + step - 1))
            return jnp.where(c < need, mi + step, mi)
        return lax.fori_loop(0, idx_bits, tie_bit, jnp.zeros((1, tq), jnp.int32))

    partial_ties = (n_ge > top_k) & (tau > KEY_NEG_INF)
    mi = lax.cond(jnp.max(partial_ties.astype(jnp.int32)) > 0, tie_search,
                  lambda: jnp.full((1, tq), seq_len, jnp.int32))

    def bias_block(j, _):
        k = key_scr[j]
        idx = j * tk + row_k
        sel = (k > tau) | ((k == tau) & (idx <= mi))
        adm = (idx // CHUNK) <= qchunk
        bias_scr[j] = jnp.where(sel & adm, 0.0, NEG)
        return 0

    lax.fori_loop(0, nkv, bias_block, 0)

    for h0 in range(0, DSA_HEADS, DSA_HEADS_PER_LOOP):
        heads = tuple(range(h0, h0 + DSA_HEADS_PER_LOOP))
        qs = [qd_ref[:, h * DSA_HEAD_DIM:(h + 1) * DSA_HEAD_DIM] for h in heads]

        def att(j, carries, heads=heads, qs=qs):
            off = pl.multiple_of(j * tk, tk)
            sls = [slice(h * DSA_HEAD_DIM, (h + 1) * DSA_HEAD_DIM) for h in heads]
            kbs = [kd_ref[pl.ds(off, tk), sl] for sl in sls]
            vtbs = [vt_ref[j, sl, :] for sl in sls]
            return _flash_t_steps(carries, qs, kbs, vtbs, bias_scr[j], s_scr)

        carries = lax.fori_loop(0, nkv, att, tuple(_flash_t_init(tq, DSA_HEAD_DIM) for _ in heads))
        for h, (_, l, acc) in zip(heads, carries):
            o_ref[:, h * DSA_HEAD_DIM:(h + 1) * DSA_HEAD_DIM] = (acc / l).T.astype(o_ref.dtype)


def _dsa_attn(qd, qi, wit, kd, vt, kia, kib, B, S, tq, tk, top_k):
    T = B * S
    nq = S // tq
    hd = DSA_HEADS * DSA_HEAD_DIM
    hi = IDX_HEADS * IDX_HEAD_DIM
    idx_bits = max(1, (S - 1).bit_length())
    assert vt.shape == (T // tk, hd, tk)
    qrow = lambda w: pl.BlockSpec((tq, w), lambda b, i: (b * nq + i, 0))
    seq = lambda w: pl.BlockSpec((S, w), lambda b, i: (b, 0))
    return pl.pallas_call(
        functools.partial(_dsa_attn_body, tq=tq, tk=tk, top_k=top_k, idx_bits=idx_bits, seq_len=S),
        grid=(B, nq),
        in_specs=[qrow(hd), qrow(hi), pl.BlockSpec((IDX_HEADS, tq), lambda b, i: (0, b * nq + i)), seq(hd),
                  pl.BlockSpec((S // tk, hd, tk), lambda b, i: (b, 0, 0)), seq(LANES), seq(LANES)],
        out_specs=qrow(hd),
        out_shape=jax.ShapeDtypeStruct((T, hd), jnp.bfloat16),
        scratch_shapes=[pltpu.VMEM((S // tk, tk, tq), jnp.int32), pltpu.VMEM((S // tk, tk, tq), jnp.float32),
                        pltpu.VMEM((DSA_HEADS_PER_LOOP, tk, tq), jnp.float32)],
        compiler_params=_cparams(("arbitrary", "arbitrary")),
        name="dsa_attn",
    )(qd, qi, wit, kd, vt, kia, kib)


def _merge_body(oa_ref, ob_ref, gc_ref, x_ref, woa_ref, wob_ref, wout_ref, bg_ref, gf_ref, wr_ref, br_ref,
                x1_ref, hn_ref, lg_ref):
    ya = jnp.dot(oa_ref[...], woa_ref[...], preferred_element_type=jnp.float32)
    yb = jnp.dot(ob_ref[...], wob_ref[...], preferred_element_type=jnp.float32)
    g = jax.nn.sigmoid(gc_ref[...].astype(jnp.float32) + bg_ref[...])
    merged = g[:, :D_MODEL] * ya + g[:, D_MODEL:] * yb
    x1 = x_ref[...] + jnp.dot(merged.astype(jnp.bfloat16), wout_ref[...], preferred_element_type=jnp.float32)
    x1_ref[...] = x1
    ms = jnp.mean(x1 * x1, axis=-1, keepdims=True)
    hn = x1 * lax.rsqrt(ms + NORM_EPS) * gf_ref[...]
    hn_ref[...] = hn.astype(hn_ref.dtype).reshape(hn_ref.shape)
    lg_ref[...] = lax.dot_general(wr_ref[...], hn.astype(jnp.bfloat16), _NT,
                                  preferred_element_type=jnp.float32) + br_ref[...]


def _merge(oa, ob, proj, x2, woa, wob, wout, bg, gf, wr_t, br, tm):
    T = x2.shape[0]
    row = lambda w: pl.BlockSpec((tm, w), lambda i: (i, 0))
    full = lambda shape: pl.BlockSpec(shape, lambda i: (0, 0))
    return pl.pallas_call(
        _merge_body,
        grid=(T // tm,),
        in_specs=[row(D_MODEL), row(D_MODEL), row(2 * D_MODEL), row(D_MODEL),
                  full((D_MODEL, D_MODEL)), full((D_MODEL, D_MODEL)), full((D_MODEL, D_MODEL)),
                  full((1, 2 * D_MODEL)), full((1, D_MODEL)), full((N_EXPERTS, D_MODEL)), full((N_EXPERTS, 1))],
        out_specs=[row(D_MODEL), pl.BlockSpec((tm,) + ROW_TILE, lambda i: (i, 0, 0)),
                   pl.BlockSpec((N_EXPERTS, tm), lambda i: (0, i))],
        out_shape=[
            jax.ShapeDtypeStruct((T, D_MODEL), jnp.float32),
            jax.ShapeDtypeStruct((T,) + ROW_TILE, ROW_DTYPE),
            jax.ShapeDtypeStruct((N_EXPERTS, T), jnp.float32),
        ],
        compiler_params=_cparams(("arbitrary",)),
        name="merge",
    )(oa, ob, proj, x2, woa, wob, wout, bg, gf, wr_t, br)


def _route_body(lg_ref, eidx_ref, gate_ref, rank_ref, cnt_ref, carry_scr, *, tr):
    @pl.when(pl.program_id(0) == 0)
    def _():
        carry_scr[...] = jnp.zeros_like(carry_scr)

    vals = lg_ref[...]
    rows = lax.broadcasted_iota(jnp.int32, vals.shape, 0)
    tops, hots = [], []
    for k in range(MOE_TOP_K):
        m = jnp.max(vals, axis=0, keepdims=True)
        idx = jnp.min(jnp.where(vals == m, rows, N_EXPERTS), axis=0, keepdims=True)
        hot = rows == idx
        vals = jnp.where(hot, -jnp.inf, vals)
        tops.append(m)
        hots.append(hot)
        eidx_ref[k:k + 1, :] = idx
    es = [jnp.exp(m - tops[0]) for m in tops]
    den = es[0] + es[1] + es[2] + es[3]
    for k in range(MOE_TOP_K):
        gate_ref[k:k + 1, :] = es[k] / den
    sel = jnp.zeros(vals.shape, jnp.float32)
    for hot in hots:
        sel = sel + hot.astype(jnp.float32)
    before = (lax.broadcasted_iota(jnp.int32, (tr, tr), 0) < lax.broadcasted_iota(jnp.int32, (tr, tr), 1))
    prefix = jnp.dot(sel.astype(jnp.bfloat16), before.astype(jnp.bfloat16), preferred_element_type=jnp.float32)
    prefix = prefix + carry_scr[:, 0:1]
    for k in range(MOE_TOP_K):
        rank_ref[k:k + 1, :] = jnp.sum(jnp.where(hots[k], prefix, 0.0), axis=0, keepdims=True).astype(jnp.int32)
    carry_scr[...] = carry_scr[...] + jnp.sum(sel, axis=1, keepdims=True)
    cnt_ref[...] = carry_scr[...]


def _route(logits_t, tr):
    T = logits_t.shape[1]
    tok = pl.BlockSpec((MOE_TOP_K, tr), lambda i: (0, i))
    return pl.pallas_call(
        functools.partial(_route_body, tr=tr),
        grid=(T // tr,),
        in_specs=[pl.BlockSpec((N_EXPERTS, tr), lambda i: (0, i))],
        out_specs=[tok, tok, tok, pl.BlockSpec((N_EXPERTS, LANES), lambda i: (0, 0))],
        out_shape=[
            jax.ShapeDtypeStruct((MOE_TOP_K, T), jnp.int32),
            jax.ShapeDtypeStruct((MOE_TOP_K, T), jnp.float32),
            jax.ShapeDtypeStruct((MOE_TOP_K, T), jnp.int32),
            jax.ShapeDtypeStruct((N_EXPERTS, LANES), jnp.float32),
        ],
        scratch_shapes=[pltpu.VMEM((N_EXPERTS, LANES), jnp.float32)],
        compiler_params=_cparams(("arbitrary",)),
        name="route",
    )(logits_t)


def _dest_body(poff_ref, eidx_ref, rank_ref, dest_ref):
    e = eidx_ref[...]
    base = jnp.zeros(e.shape, jnp.int32)
    for x in range(N_EXPERTS):
        base = jnp.where(e == x, poff_ref[x], base)
    dest_ref[...] = base + rank_ref[...]


def _dest(pad_off, eidx, rank, tr):
    T = eidx.shape[1]
    tok = pl.BlockSpec((MOE_TOP_K, tr), lambda i, p: (0, i))
    return pl.pallas_call(
        _dest_body,
        grid_spec=pltpu.PrefetchScalarGridSpec(
            num_scalar_prefetch=1, grid=(T // tr,), in_specs=[tok, tok], out_specs=tok),
        out_shape=jax.ShapeDtypeStruct((MOE_TOP_K, T), jnp.int32),
        compiler_params=_cparams(("arbitrary",)),
        name="dest",
    )(pad_off, eidx, rank)


def _dispatch_body(zstart_ref, zvalid_ref, dest_ref, dest_prev_ref, hn_ref, xs_ref, z_scr, zsem, hbuf, lsems, rsems,
                   *, tm):
    @pl.when(pl.program_id(0) == 0)
    def _():
        z_scr[...] = jnp.zeros_like(z_scr)

        def zero_copy(n):
            return pltpu.make_async_copy(z_scr, xs_ref.at[pl.ds(zstart_ref[n], MOE_BLOCK)], zsem)

        def zissue(n, _):
            @pl.when(zvalid_ref[n] > 0)
            def _():
                zero_copy(n).start()
            return 0

        def zdrain(n, _):
            @pl.when(zvalid_ref[n] > 0)
            def _():
                zero_copy(n).wait()
            return 0

        lax.fori_loop(0, 2 * N_EXPERTS, zissue, 0)
        lax.fori_loop(0, 2 * N_EXPERTS, zdrain, 0)

    i = pl.program_id(0)
    n = pl.num_programs(0)

    def tile_load(step):
        s = step % DISPATCH_SLOTS
        return pltpu.make_async_copy(hn_ref.at[pl.ds(step * tm, tm)], hbuf.at[s], lsems.at[s])

    def row_copy(dref, s, k, t):
        return pltpu.make_async_copy(hbuf.at[s, t], xs_ref.at[dref[k, t]], rsems.at[s])

    def drain_rows(dref, s):
        def drain(t, _):
            for k in range(MOE_TOP_K):
                row_copy(dref, s, k, t).wait()
            return 0
        lax.fori_loop(0, tm, drain, 0, unroll=DRAIN_UNROLL)

    @pl.when(i == 0)
    def _():
        tile_load(0).start()

        @pl.when(n > 1)
        def _():
            tile_load(1).start()

    slot = i % DISPATCH_SLOTS
    tile_load(i).wait()

    def issue(t, _):
        for k in range(MOE_TOP_K):
            row_copy(dest_ref, slot, k, t).start(priority=k % DMA_PRIORITIES)
        return 0

    lax.fori_loop(0, tm, issue, 0, unroll=ISSUE_UNROLL)

    @pl.when(i > 0)
    def _():
        drain_rows(dest_prev_ref, (i - 1) % DISPATCH_SLOTS)

    @pl.when(i + 2 < n)
    def _():
        tile_load(i + 2).start()

    @pl.when(i == n - 1)
    def _():
        drain_rows(dest_ref, slot)


def _dispatch(zero_start, zero_valid, dest, hn, P, tm):
    T = hn.shape[0]
    return pl.pallas_call(
        functools.partial(_dispatch_body, tm=tm),
        grid_spec=pltpu.PrefetchScalarGridSpec(
            num_scalar_prefetch=2, grid=(T // tm,),
            in_specs=[
                pl.BlockSpec((MOE_TOP_K, tm), lambda i, zs, zv: (0, i), memory_space=pltpu.SMEM),
                pl.BlockSpec((MOE_TOP_K, tm), lambda i, zs, zv: (0, jnp.maximum(i - 1, 0)), memory_space=pltpu.SMEM),
                pl.BlockSpec(memory_space=pl.ANY),
            ],
            out_specs=pl.BlockSpec(memory_space=pl.ANY),
            scratch_shapes=[pltpu.VMEM((MOE_BLOCK,) + ROW_TILE, hn.dtype), pltpu.SemaphoreType.DMA(()),
                            pltpu.VMEM((DISPATCH_SLOTS, tm) + ROW_TILE, hn.dtype),
                            pltpu.SemaphoreType.DMA((DISPATCH_SLOTS,)), pltpu.SemaphoreType.DMA((DISPATCH_SLOTS,))]),
        out_shape=jax.ShapeDtypeStruct((P,) + ROW_TILE, hn.dtype),
        compiler_params=_cparams(("arbitrary",)),
        name="dispatch",
    )(zero_start, zero_valid, dest, dest, hn)


def _experts_body(bexp_ref, first_ref, nused_ref, xs_ref, wgu_ref, bgu_ref, wd_ref, bd_ref, y_ref, wgu_s, wd_s):
    j = pl.program_id(0)

    @pl.when(first_ref[j] > 0)
    def _():
        wgu_s[...] = wgu_ref[0].astype(jnp.bfloat16)
        wd_s[...] = wd_ref[0].astype(jnp.bfloat16)

    @pl.when(j < nused_ref[0])
    def _():
        xb = xs_ref[...].reshape(MOE_BLOCK, D_MODEL).astype(jnp.bfloat16)
        h = jnp.dot(xb, wgu_s[...], preferred_element_type=jnp.float32) + bgu_ref[0]
        gate = jnp.minimum(h[:, :D_EXPERT], SWIGLU_LIMIT)
        lin = jnp.clip(h[:, D_EXPERT:], -SWIGLU_LIMIT, SWIGLU_LIMIT)
        glu = gate * jax.nn.sigmoid(SWIGLU_ALPHA * gate)
        act = ((lin + 1.0) * glu).astype(jnp.bfloat16)
        y = jnp.dot(act, wd_s[...], preferred_element_type=jnp.float32) + bd_ref[0]
        y_ref[...] = y.astype(y_ref.dtype).reshape(y_ref.shape)

    @pl.when(j >= nused_ref[0])
    def _():
        y_ref[...] = jnp.zeros_like(y_ref)


def _experts(blk_expert, blk_first, n_used, xs, wgu, bgu, wd, bd):
    P = xs.shape[0]
    nb = P // MOE_BLOCK
    rows = pl.BlockSpec((MOE_BLOCK,) + ROW_TILE, lambda j, be, bf, nu: (j, 0, 0))
    return pl.pallas_call(
        _experts_body,
        grid_spec=pltpu.PrefetchScalarGridSpec(
            num_scalar_prefetch=3, grid=(nb,),
            in_specs=[
                rows,
                pl.BlockSpec((1, D_MODEL, 2 * D_EXPERT), lambda j, be, bf, nu: (be[j], 0, 0)),
                pl.BlockSpec((1, 1, 2 * D_EXPERT), lambda j, be, bf, nu: (be[j], 0, 0)),
                pl.BlockSpec((1, D_EXPERT, D_MODEL), lambda j, be, bf, nu: (be[j], 0, 0)),
                pl.BlockSpec((1, 1, D_MODEL), lambda j, be, bf, nu: (be[j], 0, 0)),
            ],
            out_specs=rows,
            scratch_shapes=[pltpu.VMEM((D_MODEL, 2 * D_EXPERT), jnp.bfloat16),
                            pltpu.VMEM((D_EXPERT, D_MODEL), jnp.bfloat16)]),
        out_shape=jax.ShapeDtypeStruct((P,) + ROW_TILE, ROW_DTYPE),
        compiler_params=_cparams(("arbitrary",)),
        name="experts",
    )(blk_expert, blk_first, n_used, xs, wgu, bgu, wd, bd)


def _combine_body(dest_ref, dest_next_ref, y_ref, x1_ref, gate_ref, gn_ref, o_ref, ybuf, sems, *, tm):
    i = pl.program_id(0)
    slot = i % 2

    def row_copy(dref, s, k, t):
        return pltpu.make_async_copy(y_ref.at[dref[k, t]], ybuf.at[s, k, t], sems.at[s])

    def issue_rows(dref, s):
        def issue(t, _):
            for k in range(MOE_TOP_K):
                row_copy(dref, s, k, t).start(priority=k % DMA_PRIORITIES)
            return 0
        lax.fori_loop(0, tm, issue, 0, unroll=ISSUE_UNROLL)

    @pl.when(i == 0)
    def _():
        issue_rows(dest_ref, 0)

    @pl.when(i + 1 < pl.num_programs(0))
    def _():
        issue_rows(dest_next_ref, 1 - slot)

    def drain(t, _):
        for k in range(MOE_TOP_K):
            row_copy(dest_ref, slot, k, t).wait()
        return 0

    lax.fori_loop(0, tm, drain, 0, unroll=DRAIN_UNROLL)

    g = gate_ref[...]
    acc = x1_ref[...]
    for k in range(MOE_TOP_K):
        acc = acc + g[:, k:k + 1] * ybuf[slot, k].reshape(tm, D_MODEL).astype(jnp.float32)
    ms = jnp.mean(acc * acc, axis=-1, keepdims=True)
    o_ref[...] = acc * lax.rsqrt(ms + NORM_EPS) * gn_ref[...]


def _combine(dest, y, x1, gate_t, gn, tm):
    T = x1.shape[0]
    n = T // tm
    return pl.pallas_call(
        functools.partial(_combine_body, tm=tm),
        grid=(n,),
        in_specs=[
            pl.BlockSpec((MOE_TOP_K, tm), lambda i: (0, i), memory_space=pltpu.SMEM),
            pl.BlockSpec((MOE_TOP_K, tm), lambda i: (0, jnp.minimum(i + 1, n - 1)), memory_space=pltpu.SMEM),
            pl.BlockSpec(memory_space=pl.ANY),
            pl.BlockSpec((tm, D_MODEL), lambda i: (i, 0)),
            pl.BlockSpec((tm, MOE_TOP_K), lambda i: (i, 0)),
            pl.BlockSpec((1, D_MODEL), lambda i: (0, 0)),
        ],
        out_specs=pl.BlockSpec((tm, D_MODEL), lambda i: (i, 0)),
        out_shape=jax.ShapeDtypeStruct((T, D_MODEL), jnp.float32),
        scratch_shapes=[pltpu.VMEM((2, MOE_TOP_K, tm) + ROW_TILE, y.dtype), pltpu.SemaphoreType.DMA((2,))],
        compiler_params=_cparams(("arbitrary",)),
        name="combine",
    )(dest, dest, y, x1, gate_t, gn)


def _rope_tables(S):
    pos = jnp.arange(S, dtype=jnp.float32)[:, None]

    def cs(half):
        inv = ROPE_THETA ** (-jnp.arange(half, dtype=jnp.float32) / half)
        ang = pos * inv[None, :]
        return jnp.cos(ang), jnp.sin(ang)

    z = lambda n: jnp.zeros((S, n), jnp.float32)
    o = lambda n: jnp.ones((S, n), jnp.float32)
    cat = lambda *a: jnp.concatenate(a, axis=1)
    c, s = cs(MLA_ROPE // 2)
    mla = (cat(c, c, z(64)), cat(z(32), s, z(64)), cat(-s, z(96)))
    c, s = cs(DSA_HEAD_DIM // 2)
    dsa = (cat(c, c), cat(-s, s))
    c, s = cs(IDX_ROPE // 2)
    c1, sa1, sb1 = cat(c, c, o(32)), cat(z(16), s, z(32)), cat(-s, z(48))
    idx_q = (cat(c1, c1), cat(sa1, sa1), cat(sb1, sb1))
    idx_k = (cat(c1, o(64)), cat(sa1, z(64)), cat(sb1, z(64)))
    return mla, dsa + idx_q + idx_k


def _pack_w_in(w):
    sizes = (MLA_Q_LORA, MLA_KV_LORA, MLA_ROPE, 3 * DSA_HEADS * DSA_HEAD_DIM, IDX_HEADS * IDX_HEAD_DIM,
             IDX_HEAD_DIM, IDX_HEADS, 2 * D_MODEL)
    pts, acc = [], 0
    for s in sizes[:-1]:
        acc += s
        pts.append(acc)
    w = w.astype(jnp.bfloat16)
    c_q, c_kv, k_r, qkv, q_idx, k_idx, w_idx, gates = jnp.split(w, pts, axis=1)
    z = lambda n: jnp.zeros((w.shape[0], n), w.dtype)
    packed = jnp.concatenate([gates, qkv, q_idx, k_idx, w_idx, z(LANES - IDX_HEAD_DIM - IDX_HEADS),
                              c_q, c_kv, k_r, z(LANES - MLA_ROPE)], axis=1)
    assert packed.shape[1] == D_IN_PACKED
    return packed


def _pack_w_uq(w):
    w = w.reshape(MLA_Q_LORA, MLA_HEADS, MLA_QK)
    w = jnp.pad(w, ((0, 0), (0, 0), (0, MLA_QK_PAD - MLA_QK)))
    return w.reshape(MLA_Q_LORA, MLA_HEADS * MLA_QK_PAD).astype(jnp.bfloat16)


def _tile(n, pref):
    t = min(n, pref)
    assert n % t == 0
    return t


def kernel(x, norm_attn_g, w_in, q_lora_g, kv_lora_g, w_uq, w_uk, w_uv, w_o_mla, w_o_dsa, b_gate, w_out,
           norm_ffn_g, w_router, b_router, w_gate_up, b_gate_up, w_down, b_down, norm_final_g):
    B, S, D = x.shape
    assert D == D_MODEL and S % CHUNK == 0 and norm_attn_g.shape[0] == 1
    T = B * S
    top_k = min(DSA_TOPK_MAX, S // 4)
    bf = jnp.bfloat16
    x2 = x.reshape(T, D)
    mla_tabs, dsa_tabs = _rope_tables(S)

    proj = _in_proj(x2, norm_attn_g, _pack_w_in(w_in[0]), _tile(T, 1024), D_IN_PACKED // 3)
    tk = _tile(S, 512)
    tq = _tile(S, 512)
    q, k, vt = _mla_prep(proj, q_lora_g, kv_lora_g, _pack_w_uq(w_uq[0]), w_uk[0].astype(bf), w_uv[0].T.astype(bf),
                         mla_tabs, S, tk)
    o_mla = _mla_attn(q, k, vt, B, S, tq, tk)
    qd, kd, vtd, qi, kia, kib, wit = _dsa_prep(proj, dsa_tabs, S, tk)
    o_dsa = _dsa_attn(qd, qi, wit, kd, vtd, kia, kib, B, S, tq, tk, top_k)
    x1, hn, logits_t = _merge(o_mla, o_dsa, proj, x2, w_o_mla[0].astype(bf), w_o_dsa[0].astype(bf),
                              w_out[0].astype(bf), b_gate, norm_ffn_g, w_router[0].T.astype(bf),
                              b_router[0][:, None], _tile(T, 512))

    tr = _tile(T, 512)
    eidx, gate, rank, cnt = _route(logits_t, tr)
    counts = cnt[:, 0].astype(jnp.int32)
    padded = ((counts + MOE_BLOCK - 1) // MOE_BLOCK) * MOE_BLOCK
    pad_end = jnp.cumsum(padded)
    pad_off = pad_end - padded
    P = T * MOE_TOP_K + N_EXPERTS * MOE_BLOCK
    nb = P // MOE_BLOCK
    blk_start = jnp.arange(nb, dtype=jnp.int32) * MOE_BLOCK
    blk_expert = jnp.minimum(jnp.sum(pad_end[None, :] <= blk_start[:, None], axis=1), N_EXPERTS - 1).astype(jnp.int32)
    blk_first = jnp.concatenate([jnp.ones((1,), jnp.int32), (blk_expert[1:] != blk_expert[:-1]).astype(jnp.int32)])
    n_used = (pad_end[-1:] // MOE_BLOCK).astype(jnp.int32)

    dest = _dest(pad_off.astype(jnp.int32), eidx, rank, tr)
    trailing = n_used[0] + jnp.arange(N_EXPERTS, dtype=jnp.int32)
    zero_start = jnp.concatenate([pad_end - MOE_BLOCK, jnp.minimum(trailing, nb - 1) * MOE_BLOCK]).astype(jnp.int32)
    zero_valid = jnp.concatenate([padded > 0, trailing < nb]).astype(jnp.int32)
    xs = _dispatch(zero_start, zero_valid, dest, hn, P, _tile(T, 128))
    y = _experts(blk_expert, blk_first, n_used, xs, w_gate_up[0], b_gate_up[0][:, None, :],
                 w_down[0], b_down[0][:, None, :])
    out = _combine(dest, y, x1, gate.T, norm_final_g[None, :], _tile(T, 128))
    return out.reshape(B, S, D)
```
